```python
import math
import jax, jax.numpy as jnp
from jax import lax
import numpy as np

D_MODEL = 2048
BATCH = 8
SEQ = 2048
DEPTH = 1

GLA_HEADS = 4
GLA_DK = D_MODEL // 2 // GLA_HEADS
GLA_DV = D_MODEL // GLA_HEADS
GLA_GATE_RANK = 16
GLA_GATE_NORM = 16.0
GDN_HEADS = 16
GDN_DK = D_MODEL // GDN_HEADS
GDN_DV = D_MODEL // GDN_HEADS
CONV_K = 4
CHUNK = 64
N_BRANCH = 2
BRANCH_W = D_MODEL
NORM_EPS = 1e-6
LN_EPS = 1e-5
DEEPNORM_ALPHA = (2 * DEPTH) ** 0.25
DEEPNORM_BETA = (8 * DEPTH) ** -0.25

GLA_QK = GLA_HEADS * GLA_DK
GLA_V = GLA_HEADS * GLA_DV
GDN_QK = GDN_HEADS * GDN_DK
GDN_V = GDN_HEADS * GDN_DV
IN_SPLITS = (GLA_QK, GLA_QK, GLA_V, GLA_V, GLA_GATE_RANK,
             GDN_QK, GDN_QK, GDN_V, GDN_V, GDN_HEADS, GDN_HEADS,
             N_BRANCH * D_MODEL)
IN_IS_VALUE = (False, False, True, False, False,
               False, False, True, False, False, False,
               False)
IN_TOTAL = sum(IN_SPLITS)
CONV_CH = 2 * GDN_QK + GDN_V

kernel_name = "gla_gdn_gated_parallel_deepnorm_adaln"


def _split_offsets():
    offs, acc = [], 0
    for n in IN_SPLITS[:-1]:
        acc += n
        offs.append(acc)
    return offs


def _heads(t, n_heads):
    b, s, _ = t.shape
    return t.reshape(b, s, n_heads, -1).transpose(0, 2, 1, 3)


def _to_chunks(t):
    b, h, s, d = t.shape
    return t.reshape(b, h, s // CHUNK, CHUNK, d).transpose(2, 0, 1, 3, 4)


def _from_chunks(t):
    n, b, h, c, d = t.shape
    return t.transpose(1, 2, 0, 3, 4).reshape(b, h, n * c, d)


def _scalar_chunks(t):
    b, h, s = t.shape
    return t.reshape(b, h, s // CHUNK, CHUNK).transpose(2, 0, 1, 3)


def _l2norm(t):
    return t * lax.rsqrt(jnp.sum(t * t, axis=-1, keepdims=True) + NORM_EPS)


def _gated_rmsnorm(o, w, z):
    b, h, s, dv = o.shape
    o = o.transpose(0, 2, 1, 3)
    o = o * lax.rsqrt(jnp.mean(o * o, axis=-1, keepdims=True) + NORM_EPS) * w.astype(jnp.float32)
    o = o * jax.nn.silu(z.astype(jnp.float32)).reshape(b, s, h, dv)
    return o.reshape(b, s, h * dv)


def _layernorm(r, g, b):
    r = r.astype(jnp.float32)
    mu = jnp.mean(r, axis=-1, keepdims=True)
    var = jnp.mean(jnp.square(r - mu), axis=-1, keepdims=True)
    return (r - mu) * lax.rsqrt(var + LN_EPS) * g.astype(jnp.float32) + b.astype(jnp.float32)


def _causal_dwconv(u, w):
    k = w.shape[0]
    return lax.conv_general_dilated(u, w.astype(u.dtype)[:, None, :], window_strides=(1,),
                                    padding=[(k - 1, 0)], dimension_numbers=("NWC", "WIO", "NWC"),
                                    feature_group_count=u.shape[-1])


def _gla_chunked(q, k, v, log_a):
    b, h, s, dk = q.shape
    dv = v.shape[-1]
    q = q * dk ** -0.5
    causal = jnp.tril(jnp.ones((CHUNK, CHUNK), dtype=bool))

    def step(state, inp):
        q_c, k_c, v_c, g_c = inp
        cum = jnp.cumsum(g_c, axis=2)
        diff = cum[:, :, :, None, :] - cum[:, :, None, :, :]
        decay = jnp.exp(jnp.where(causal[:, :, None], diff, -jnp.inf))
        attn = jnp.einsum('bhtd,bhsd,bhtsd->bhts', q_c, k_c, decay)
        o = (jnp.einsum('bhtd,bhdv->bhtv', q_c * jnp.exp(cum), state)
             + jnp.einsum('bhts,bhsv->bhtv', attn, v_c))
        last = cum[:, :, -1:, :]
        k_dec = k_c * jnp.exp(last - cum)
        state = state * jnp.exp(last)[:, :, 0, :, None] + jnp.einsum('bhsd,bhsv->bhdv', k_dec, v_c)
        return state, o

    s0 = jnp.zeros((b, h, dk, dv), q.dtype)
    _, o = lax.scan(step, s0, (_to_chunks(q), _to_chunks(k), _to_chunks(v), _to_chunks(log_a)))
    return _from_chunks(o)


def _gdn_chunked(q, k, v, beta, g):
    dk = q.shape[-1]
    dv = v.shape[-1]
    q = q * dk ** -0.5
    qc, kc, vc = _to_chunks(q), _to_chunks(k), _to_chunks(v)
    bc, gc = _scalar_chunks(beta), _scalar_chunks(g)
    cum = jnp.cumsum(gc, axis=-1)
    causal = jnp.tril(jnp.ones((CHUNK, CHUNK), dtype=bool))
    strict = jnp.tril(jnp.ones((CHUNK, CHUNK), dtype=bool), -1)
    decay = jnp.exp(jnp.where(causal, cum[..., :, None] - cum[..., None, :], -jnp.inf))
    kb = kc * bc[..., None]
    vb = vc * bc[..., None]
    m = jnp.where(strict, jnp.einsum('nbhtd,nbhsd->nbhts', kb, kc) * decay, 0.0)
    lmat = m + jnp.eye(CHUNK, dtype=m.dtype)
    rhs = jnp.concatenate([vb, kb * jnp.exp(cum)[..., None]], axis=-1)
    sol = lax.linalg.triangular_solve(lmat, rhs, left_side=True, lower=True, unit_diagonal=True)
    u_c, w_c = sol[..., :dv], sol[..., dv:]
    qk = jnp.einsum('nbhtd,nbhsd->nbhts', qc, kc) * decay
    q_dec = qc * jnp.exp(cum)[..., None]
    k_dec = kc * jnp.exp(cum[..., -1:] - cum)[..., None]
    g_last = jnp.exp(cum[..., -1])

    def step(state, inp):
        qk_c, qd_c, kd_c, uu, ww, gl = inp
        v_new = uu - jnp.einsum('bhtd,bhdv->bhtv', ww, state)
        o = jnp.einsum('bhtd,bhdv->bhtv', qd_c, state) + jnp.einsum('bhts,bhsv->bhtv', qk_c, v_new)
        state = state * gl[..., None, None] + jnp.einsum('bhsd,bhsv->bhdv', kd_c, v_new)
        return state, o

    n, b, h = g_last.shape
    s0 = jnp.zeros((b, h, dk, dv), q.dtype)
    _, o = lax.scan(step, s0, (qk, q_dec, k_dec, u_c, w_c, g_last))
    return _from_chunks(o)


def _hybrid_layer(x, c, w_ada, b_ada, w_in, w_gk2, b_gk2, conv_w, a_log, dt_bias,
                  gla_norm_w, gdn_norm_w, w_branch, w_out, ln_g, ln_b):
    f32 = jnp.float32
    dt = x.dtype
    ada = jax.nn.silu(c) @ w_ada + b_ada
    shift, scale, gate = jnp.split(ada, 3, axis=-1)
    h = x * (1 + scale[:, None, :]) + shift[:, None, :]
    u = h @ w_in
    (gla_q, gla_k, gla_v, gla_g, gla_lr, gdn_q, gdn_k, gdn_v, gdn_z, gdn_b, gdn_a,
     merge_g) = jnp.split(u, _split_offsets(), axis=-1)

    log_a = jax.nn.log_sigmoid((gla_lr @ w_gk2 + b_gk2).astype(f32)) / GLA_GATE_NORM
    o_gla = _gla_chunked(_heads(gla_q.astype(f32), GLA_HEADS), _heads(gla_k.astype(f32), GLA_HEADS),
                         _heads(gla_v.astype(f32), GLA_HEADS), _heads(log_a, GLA_HEADS))
    y_gla = _gated_rmsnorm(o_gla, gla_norm_w, gla_g).astype(dt)

    qkv = jax.nn.silu(_causal_dwconv(jnp.concatenate([gdn_q, gdn_k, gdn_v], axis=-1), conv_w))
    qkv = qkv.astype(f32)
    q = _l2norm(_heads(qkv[..., :GDN_QK], GDN_HEADS))
    k = _l2norm(_heads(qkv[..., GDN_QK:2 * GDN_QK], GDN_HEADS))
    v = _heads(qkv[..., 2 * GDN_QK:], GDN_HEADS)
    beta = jax.nn.sigmoid(gdn_b.astype(f32)).transpose(0, 2, 1)
    g = (-jnp.exp(a_log.astype(f32))
         * jax.nn.softplus(gdn_a.astype(f32) + dt_bias.astype(f32))).transpose(0, 2, 1)
    o_gdn = _gdn_chunked(q, k, v, beta, g)
    y_gdn = _gated_rmsnorm(o_gdn, gdn_norm_w, gdn_z).astype(dt)

    g_a, g_b = jnp.split(jax.nn.sigmoid(merge_g), 2, axis=-1)
    merged = g_a * (y_gla @ w_branch[0]) + g_b * (y_gdn @ w_branch[1])
    out = merged @ w_out
    r = DEEPNORM_ALPHA * x + gate[:, None, :] * out
    return _layernorm(r, ln_g, ln_b).astype(dt)


def setup_inputs(seed: int = 0) -> dict:
    key = jax.random.key(seed)
    ks = jax.random.split(key, 18)
    f32 = jnp.float32

    def nrm(k, shape, fan_in):
        return jax.random.normal(k, shape, f32) * fan_in ** -0.5

    x = jax.random.normal(ks[0], (BATCH, SEQ, D_MODEL), f32)
    c = jax.random.normal(ks[1], (BATCH, D_MODEL), f32)
    w_ada = nrm(ks[2], (DEPTH, D_MODEL, 3 * D_MODEL), D_MODEL)
    b_ada = 0.02 * jax.random.normal(ks[3], (DEPTH, 3 * D_MODEL), f32)
    col_scale = jnp.concatenate([jnp.full((n,), DEEPNORM_BETA if isv else 1.0, f32)
                                 for n, isv in zip(IN_SPLITS, IN_IS_VALUE)])
    w_in = nrm(ks[4], (DEPTH, D_MODEL, IN_TOTAL), D_MODEL) * col_scale
    w_gk2 = nrm(ks[5], (DEPTH, GLA_GATE_RANK, GLA_QK), GLA_GATE_RANK)
    b_gk2 = 0.1 * jax.random.normal(ks[6], (DEPTH, GLA_QK), f32)
    conv_w = nrm(ks[7], (DEPTH, CONV_K, CONV_CH), CONV_K)
    a_log = jnp.log(jax.random.uniform(ks[8], (DEPTH, GDN_HEADS), f32, 1.0, 16.0))
    dt0 = jnp.exp(jax.random.uniform(ks[9], (DEPTH, GDN_HEADS), f32, math.log(1e-3), math.log(1e-1)))
    dt_bias = dt0 + jnp.log(-jnp.expm1(-dt0))
    gla_norm_w = 1.0 + 0.02 * jax.random.normal(ks[10], (DEPTH, GLA_DV), f32)
    gdn_norm_w = 1.0 + 0.02 * jax.random.normal(ks[11], (DEPTH, GDN_DV), f32)
    w_branch = nrm(ks[12], (DEPTH, N_BRANCH, BRANCH_W, D_MODEL), BRANCH_W) * DEEPNORM_BETA
    w_out = nrm(ks[13], (DEPTH, D_MODEL, D_MODEL), D_MODEL) * DEEPNORM_BETA
    ln_g = 1.0 + 0.02 * jax.random.normal(ks[14], (DEPTH, D_MODEL), f32)
    ln_b = 0.02 * jax.random.normal(ks[15], (DEPTH, D_MODEL), f32)
    return {"x": x, "c": c, "w_ada": w_ada, "b_ada": b_ada, "w_in": w_in, "w_gk2": w_gk2,
            "b_gk2": b_gk2, "conv_w": conv_w, "a_log": a_log, "dt_bias": dt_bias,
            "gla_norm_w": gla_norm_w, "gdn_norm_w": gdn_norm_w, "w_branch": w_branch,
            "w_out": w_out, "ln_g": ln_g, "ln_b": ln_b}


def reference(x, c, w_ada, b_ada, w_in, w_gk2, b_gk2, conv_w, a_log, dt_bias,
              gla_norm_w, gdn_norm_w, w_branch, w_out, ln_g, ln_b):
    for l in range(DEPTH):
        x = _hybrid_layer(x, c, w_ada[l], b_ada[l], w_in[l], w_gk2[l], b_gk2[l], conv_w[l],
                          a_log[l], dt_bias[l], gla_norm_w[l], gdn_norm_w[l], w_branch[l],
                          w_out[l], ln_g[l], ln_b[l])
    return x
```

```python
import functools

import jax
import jax.numpy as jnp
from jax import lax
from jax.experimental import pallas as pl
from jax.experimental.pallas import tpu as pltpu

F32 = jnp.float32
BF16 = jnp.bfloat16

D_MODEL = 2048
GLA_HEADS = 4
GLA_DK = 256
GLA_DV = 512
GLA_RANK = 16
GLA_GATE_NORM = 16.0
GDN_HEADS = 16
GDN_DK = 128
GDN_DV = 128
CONV_K = 4
NORM_EPS = 1e-6
LN_EPS = 1e-5

GLA_CHUNK = 64
GDN_CHUNK = 128

OFF_GLA_Q = 0
OFF_GLA_K = 1024
OFF_GLA_V = 2048
OFF_GLA_G = 4096
OFF_GDN_Q = 6144
OFF_GDN_K = 8192
OFF_GDN_V = 10240
OFF_GDN_Z = 12288
OFF_MERGE = 14336
N_MAIN = 18432
N_SMALL = 128
LANE_B = 16
LANE_A = 32

VMEM_LIMIT = 56 * 1024 * 1024


def _sigmoid(x):
    return 1.0 / (1.0 + jnp.exp(-x))


def _softplus(x):
    return jnp.maximum(x, 0.0) + jnp.log1p(jnp.exp(-jnp.abs(x)))


def _dot(a, b):
    return jnp.dot(a, b, preferred_element_type=F32)


def _dot_nt(a, b):
    return lax.dot_general(a, b, (((1,), (1,)), ((), ())), preferred_element_type=F32)


def _dot_tn(a, b):
    return lax.dot_general(a, b, (((0,), (0,)), ((), ())), preferred_element_type=F32)


def _same_chunk(row, col, chunk):
    shift = chunk.bit_length() - 1
    return jnp.right_shift(row, shift) == jnp.right_shift(col, shift)


def _split3(x):
    hi = x.astype(BF16)
    r1 = x - hi.astype(F32)
    lo = r1.astype(BF16)
    lo2 = (r1 - lo.astype(F32)).astype(BF16)
    return hi, lo, lo2


def _dot_exact_lhs(m_bf16, x):
    hi, lo, lo2 = _split3(x)
    return _dot(m_bf16, hi) + _dot(m_bf16, lo) + _dot(m_bf16, lo2)


def _dot_exact_rhs(x, m_bf16):
    hi, lo, lo2 = _split3(x)
    return _dot(hi, m_bf16) + _dot(lo, m_bf16) + _dot(lo2, m_bf16)


def _ada_kernel(c_ref, w_ref, b_ref, o_ref):
    c = c_ref[...]
    s = (c * _sigmoid(c)).astype(BF16)
    o_ref[...] = _dot(s, w_ref[...].astype(BF16)) + b_ref[...]


def _ada(c, w_ada, b_ada):
    bsz, d = c.shape
    n = w_ada.shape[1]
    tn = 768
    return pl.pallas_call(
        _ada_kernel,
        grid=(n // tn,),
        in_specs=[
            pl.BlockSpec((bsz, d), lambda j: (0, 0)),
            pl.BlockSpec((d, tn), lambda j: (0, j)),
            pl.BlockSpec((1, tn), lambda j: (0, j)),
        ],
        out_specs=pl.BlockSpec((bsz, tn), lambda j: (0, j)),
        out_shape=jax.ShapeDtypeStruct((bsz, n), F32),
        compiler_params=pltpu.CompilerParams(
            dimension_semantics=("arbitrary",), vmem_limit_bytes=VMEM_LIMIT),
        name="ada",
    )(c, w_ada, b_ada.reshape(1, n))


def _inproj_kernel(x_ref, sc_ref, sh_ref, w_ref, ws_ref, u_ref, sm_ref, h_scr, *, tm):
    @pl.when(pl.program_id(1) == 0)
    def _():
        sc = 1.0 + sc_ref[0]
        sh = sh_ref[0]
        rb = 256
        for r in range(tm // rb):
            h = x_ref[r * rb:(r + 1) * rb, :] * sc + sh
            h_scr[r * rb:(r + 1) * rb, :] = h.astype(BF16)
        sm_ref[...] = _dot(h_scr[...], ws_ref[...])

    u_ref[...] = _dot(h_scr[...], w_ref[...]).astype(u_ref.dtype)


def _inproj(x2, ada3, w_main, w_small, seq):
    t, d = x2.shape
    tm, tn = 1024, 1024
    per_b = seq // tm
    return pl.pallas_call(
        functools.partial(_inproj_kernel, tm=tm),
        grid=(t // tm, N_MAIN // tn),
        in_specs=[
            pl.BlockSpec((tm, d), lambda m, n: (m, 0)),
            pl.BlockSpec((1, 1, d), lambda m, n: (m // per_b, 0, 1)),
            pl.BlockSpec((1, 1, d), lambda m, n: (m // per_b, 0, 0)),
            pl.BlockSpec((d, tn), lambda m, n: (0, n)),
            pl.BlockSpec((d, N_SMALL), lambda m, n: (0, 0)),
        ],
        out_specs=[
            pl.BlockSpec((tm, tn), lambda m, n: (m, n)),
            pl.BlockSpec((tm, N_SMALL), lambda m, n: (m, 0)),
        ],
        out_shape=[
            jax.ShapeDtypeStruct((t, N_MAIN), BF16),
            jax.ShapeDtypeStruct((t, N_SMALL), F32),
        ],
        scratch_shapes=[pltpu.VMEM((tm, d), BF16)],
        compiler_params=pltpu.CompilerParams(
            dimension_semantics=("arbitrary", "arbitrary"), vmem_limit_bytes=VMEM_LIMIT),
        name="inproj",
    )(x2, ada3, ada3, w_main, w_small)


def _gla_kernel(q_ref, k_ref, v_ref, g_ref, sm_ref, w2_ref, b2_ref, nw_ref, y_ref,
                qe_s, ki_s, kd_s, dl_s, st_s, *, seq):
    c = GLA_CHUNK
    rb = 256
    row = lax.broadcasted_iota(jnp.int32, (rb, rb), 0)
    col = lax.broadcasted_iota(jnp.int32, (rb, rb), 1)
    same = _same_chunk(row, col, c)
    l_incl = jnp.where(same & (col <= row), 1.0, 0.0).astype(BF16)
    l_full = jnp.where(same, 1.0, 0.0).astype(BF16)
    scale = GLA_DK ** -0.5

    def prep(i, carry):
        r0 = pl.multiple_of(i * rb, rb)
        rows = pl.ds(r0, rb)
        z = _dot(sm_ref[rows, :].astype(BF16), w2_ref[...]) + b2_ref[...]
        la = -_softplus(-z) * (1.0 / GLA_GATE_NORM)
        hi, lo, lo2 = _split3(la)
        cum = _dot(l_incl, hi) + _dot(l_incl, lo) + _dot(l_incl, lo2)
        last = _dot(l_full, hi) + _dot(l_full, lo) + _dot(l_full, lo2)
        q = q_ref[rows, :].astype(F32)
        k = k_ref[rows, :].astype(F32)
        qe_s[rows, :] = (q * (jnp.exp(cum) * scale)).astype(BF16)
        ki_s[rows, :] = (k * jnp.exp(-cum)).astype(BF16)
        kd_s[rows, :] = (k * jnp.exp(last - cum)).astype(BF16)
        dl_s[rows, :] = jnp.exp(last)
        return carry

    lax.fori_loop(0, seq // rb, prep, 0)

    st_s[...] = jnp.zeros_like(st_s)
    tr = lax.broadcasted_iota(jnp.int32, (c, c), 0)
    tc = lax.broadcasted_iota(jnp.int32, (c, c), 1)
    causal = tc <= tr

    def step(n, carry):
        r0 = pl.multiple_of(n * c, c)
        rows = pl.ds(r0, c)
        qe = qe_s[rows, :]
        ki = ki_s[rows, :]
        kd = kd_s[rows, :]
        v = v_ref[rows, :].astype(BF16)
        attn = jnp.where(causal, _dot_nt(qe, ki), 0.0).astype(BF16)
        st = st_s[...]
        o = _dot_nt(qe, st.astype(BF16)) + _dot(attn, v)
        dec = dl_s[pl.ds(r0, 8), :][0:1, :]
        st_s[...] = st * dec + _dot_tn(v, kd)
        ms = jnp.mean(o * o, axis=-1, keepdims=True)
        g = g_ref[rows, :].astype(F32)
        y = o * lax.rsqrt(ms + NORM_EPS) * nw_ref[...] * (g * _sigmoid(g))
        y_ref[rows, :] = y.astype(y_ref.dtype)
        return carry

    lax.fori_loop(0, seq // c, step, 0)


def _gla(u, small, w2p, b2, norm_w, bsz, seq):
    t = u.shape[0]
    qb, vb = GLA_DK, GLA_DV
    return pl.pallas_call(
        functools.partial(_gla_kernel, seq=seq),
        grid=(bsz, GLA_HEADS),
        in_specs=[
            pl.BlockSpec((seq, qb), lambda b, h: (b, OFF_GLA_Q // qb + h)),
            pl.BlockSpec((seq, qb), lambda b, h: (b, OFF_GLA_K // qb + h)),
            pl.BlockSpec((seq, vb), lambda b, h: (b, OFF_GLA_V // vb + h)),
            pl.BlockSpec((seq, vb), lambda b, h: (b, OFF_GLA_G // vb + h)),
            pl.BlockSpec((seq, N_SMALL), lambda b, h: (b, 0)),
            pl.BlockSpec((N_SMALL, qb), lambda b, h: (0, h)),
            pl.BlockSpec((1, qb), lambda b, h: (0, h)),
            pl.BlockSpec((1, vb), lambda b, h: (0, 0)),
        ],
        out_specs=pl.BlockSpec((seq, vb), lambda b, h: (b, h)),
        out_shape=jax.ShapeDtypeStruct((t, GLA_HEADS * GLA_DV), BF16),
        scratch_shapes=[
            pltpu.VMEM((seq, qb), BF16),
            pltpu.VMEM((seq, qb), BF16),
            pltpu.VMEM((seq, qb), BF16),
            pltpu.VMEM((seq, qb), F32),
            pltpu.VMEM((vb, qb), F32),
        ],
        compiler_params=pltpu.CompilerParams(
            dimension_semantics=("arbitrary", "arbitrary"), vmem_limit_bytes=VMEM_LIMIT),
        name="gla",
    )(u, u, u, u, small, w2p, b2, norm_w)


INV_BASE = 8


def _unit_lower_inverse(m, eye, tr, tc):
    n = m.shape[0]
    x = jnp.where(_same_chunk(tr, tc, INV_BASE), -m, 0.0)
    tinv = eye + x
    p = x
    size = 2
    while size < INV_BASE:
        pb = p.astype(BF16)
        p = _dot(pb, pb)
        tinv = tinv + _dot(tinv.astype(BF16), p.astype(BF16))
        size *= 2
    blk = INV_BASE
    while blk < n:
        off = _same_chunk(tr, tc, 2 * blk) & jnp.logical_not(_same_chunk(tr, tc, blk))
        moff = jnp.where(off, m, 0.0).astype(BF16)
        tb = tinv.astype(BF16)
        tinv = tinv - _dot(_dot(tb, moff).astype(BF16), tb)
        blk *= 2
    return tinv


def _gdn_kernel(q_ref, k_ref, v_ref, z_ref, sm_ref, cwq_ref, cwk_ref, cwv_ref, al_ref, dt_ref,
                nw_ref, y_ref,
                xq_s, xk_s, xv_s, q_s, k_s, v_s, bc_s, cc_s,
                u_s, w_s, a_s, qd_s, kdt_s, gl_s, st_s, *, seq):
    c = GDN_CHUNK
    nchunk = seq // c
    h = pl.program_id(1)
    rb = 256
    scale = GDN_DK ** -0.5

    for src, dst in ((q_ref, xq_s), (k_ref, xk_s), (v_ref, xv_s)):
        dst[0:8, :] = jnp.zeros((8, GDN_DK), F32)
        dst[8:8 + seq, :] = src[...].astype(F32)

    row = lax.broadcasted_iota(jnp.int32, (rb, rb), 0)
    col = lax.broadcasted_iota(jnp.int32, (rb, rb), 1)
    l_incl = jnp.where(_same_chunk(row, col, c) & (col <= row), 1.0, 0.0).astype(BF16)
    er =lax.broadcasted_iota(jnp.int32, (N_SMALL, GDN_DK), 0)
    sel_b = jnp.where(er == LANE_B + h, 1.0, 0.0).astype(BF16)
    sel_a = jnp.where(er == LANE_A + h, 1.0, 0.0).astype(BF16)
    neg_a = -jnp.exp(al_ref[...])
    dtb = dt_ref[...]

    def conv_silu(xs, cw_ref, r0):
        acc = None
        for j in range(CONV_K):
            tap = xs[r0 + 8 - (CONV_K - 1) + j: r0 + 8 - (CONV_K - 1) + j + rb, :] * cw_ref[j:j + 1, :]
            acc = tap if acc is None else acc + tap
        return acc * _sigmoid(acc)

    def l2n(x):
        return x * lax.rsqrt(jnp.sum(x * x, axis=-1, keepdims=True) + NORM_EPS)

    for i in range(seq // rb):
        r0 = i * rb
        rows = slice(r0, r0 + rb)
        q_s[rows, :] = l2n(conv_silu(xq_s, cwq_ref, r0)) * scale
        k_s[rows, :] = l2n(conv_silu(xk_s, cwk_ref, r0))
        v_s[rows, :] = conv_silu(xv_s, cwv_ref, r0)
        sm = sm_ref[rows, :]
        beta_all = _sigmoid(sm)
        g_all = neg_a * _softplus(sm + dtb)
        bc_s[rows, :] = _dot_exact_rhs(beta_all, sel_b)
        gcol = _dot_exact_rhs(g_all, sel_a)
        cc_s[rows, :] = _dot_exact_lhs(l_incl, gcol)

    tr = lax.broadcasted_iota(jnp.int32, (c, c), 0)
    tc = lax.broadcasted_iota(jnp.int32, (c, c), 1)
    lower = tc <= tr
    strict = tc < tr
    eye = jnp.where(tc == tr, 1.0, 0.0).astype(F32)

    def chunk_prep(n, carry):
        r0 = pl.multiple_of(n * c, c)
        rows = pl.ds(r0, c)
        q = q_s[rows, :]
        k = k_s[rows, :]
        v = v_s[rows, :]
        bcol = bc_s[rows, :]
        ccol = cc_s[rows, :]
        brow = bcol.T
        crow = ccol.T
        gam = jnp.where(lower, jnp.exp(jnp.where(lower, ccol - crow, 0.0)), 0.0)
        kb = k.astype(BF16)
        kk = _dot_nt(kb, kb)
        qk = _dot_nt(q.astype(BF16), kb)
        m = jnp.where(strict, bcol * kk * gam, 0.0)
        tinv = _unit_lower_inverse(m, eye, tr, tc)
        tb = tinv * brow
        u_s[n] = _dot(tb.astype(BF16), v.astype(BF16))
        w_s[n] = _dot((tb * jnp.exp(crow)).astype(BF16), kb).astype(BF16)
        a_s[n] = (qk * gam).astype(BF16)
        qd_s[n] = (q * jnp.exp(ccol)).astype(BF16)
        lastrow = ccol[c - 1:c, :]
        kd = k * jnp.exp(lastrow - ccol)
        kdt_s[n] = kd.T.astype(BF16)
        gl_s[n] = jnp.broadcast_to(jnp.exp(lastrow), (8, GDN_DV))
        return carry

    lax.fori_loop(0, nchunk, chunk_prep, 0)

    st_s[...] = jnp.zeros_like(st_s)

    def step(n, carry):
        r0 = pl.multiple_of(n * c, c)
        rows = pl.ds(r0, c)
        st = st_s[...]
        sb = st.astype(BF16)
        v_new = u_s[n] - _dot(w_s[n], sb)
        vb = v_new.astype(BF16)
        o = _dot(qd_s[n], sb) + _dot(a_s[n], vb)
        st_s[...] = st * gl_s[n][0:1, :] + _dot(kdt_s[n], vb)
        ms = jnp.mean(o * o, axis=-1, keepdims=True)
        z = z_ref[rows, :].astype(F32)
        y = o * lax.rsqrt(ms + NORM_EPS) * nw_ref[...] * (z * _sigmoid(z))
        y_ref[rows, :] = y.astype(y_ref.dtype)
        return carry

    lax.fori_loop(0, nchunk, step, 0)


def _gdn(u, small, cw8, alog_row, dt_row, norm_w, bsz, seq):
    t = u.shape[0]
    dk = GDN_DK
    nchunk = seq // GDN_CHUNK
    c = GDN_CHUNK
    return pl.pallas_call(
        functools.partial(_gdn_kernel, seq=seq),
        grid=(bsz, GDN_HEADS),
        in_specs=[
            pl.BlockSpec((seq, dk), lambda b, h: (b, OFF_GDN_Q // dk + h)),
            pl.BlockSpec((seq, dk), lambda b, h: (b, OFF_GDN_K // dk + h)),
            pl.BlockSpec((seq, dk), lambda b, h: (b, OFF_GDN_V // dk + h)),
            pl.BlockSpec((seq, dk), lambda b, h: (b, OFF_GDN_Z // dk + h)),
            pl.BlockSpec((seq, N_SMALL), lambda b, h: (b, 0)),
            pl.BlockSpec((8, dk), lambda b, h: (0, h)),
            pl.BlockSpec((8, dk), lambda b, h: (0, GDN_HEADS + h)),
            pl.BlockSpec((8, dk), lambda b, h: (0, 2 * GDN_HEADS + h)),
            pl.BlockSpec((1, N_SMALL), lambda b, h: (0, 0)),
            pl.BlockSpec((1, N_SMALL), lambda b, h: (0, 0)),
            pl.BlockSpec((1, dk), lambda b, h: (0, 0)),
        ],
        out_specs=pl.BlockSpec((seq, dk), lambda b, h: (b, h)),
        out_shape=jax.ShapeDtypeStruct((t, GDN_HEADS * GDN_DV), BF16),
        scratch_shapes=[
            pltpu.VMEM((seq + 8, dk), F32),
            pltpu.VMEM((seq + 8, dk), F32),
            pltpu.VMEM((seq + 8, dk), F32),
            pltpu.VMEM((seq, dk), F32),
            pltpu.VMEM((seq, dk), F32),
            pltpu.VMEM((seq, dk), F32),
            pltpu.VMEM((seq, dk), F32),
            pltpu.VMEM((seq, dk), F32),
            pltpu.VMEM((nchunk, c, dk), F32),
            pltpu.VMEM((nchunk, c, dk), BF16),
            pltpu.VMEM((nchunk, c, c), BF16),
            pltpu.VMEM((nchunk, c, dk), BF16),
            pltpu.VMEM((nchunk, dk, c), BF16),
            pltpu.VMEM((nchunk, 8, dk), F32),
            pltpu.VMEM((dk, dk), F32),
        ],
        compiler_params=pltpu.CompilerParams(
            dimension_semantics=("arbitrary", "arbitrary"), vmem_limit_bytes=VMEM_LIMIT),
        name="gdn",
    )(u, u, u, u, small, cw8, cw8, cw8, alog_row, dt_row, norm_w)


def _merge_kernel(yg_ref, yd_ref, wa_ref, wb_ref, ga_ref, gb_ref, o_ref):
    pa = _dot(yg_ref[...], wa_ref[...])
    pb = _dot(yd_ref[...], wb_ref[...])
    ga = _sigmoid(ga_ref[...].astype(F32))
    gb = _sigmoid(gb_ref[...].astype(F32))
    o_ref[...] = (ga * pa + gb * pb).astype(o_ref.dtype)


def _merge(y_gla, y_gdn, wb0, wb1, u):
    t, d = y_gla.shape
    tm, tn = 1024, 512
    return pl.pallas_call(
        _merge_kernel,
        grid=(t // tm, d // tn),
        in_specs=[
            pl.BlockSpec((tm, d), lambda m, n: (m, 0)),
            pl.BlockSpec((tm, d), lambda m, n: (m, 0)),
            pl.BlockSpec((d, tn), lambda m, n: (0, n)),
            pl.BlockSpec((d, tn), lambda m, n: (0, n)),
            pl.BlockSpec((tm, tn), lambda m, n: (m, OFF_MERGE // tn + n)),
            pl.BlockSpec((tm, tn), lambda m, n: (m, (OFF_MERGE + D_MODEL) // tn + n)),
        ],
        out_specs=pl.BlockSpec((tm, tn), lambda m, n: (m, n)),
        out_shape=jax.ShapeDtypeStruct((t, d), BF16),
        compiler_params=pltpu.CompilerParams(
            dimension_semantics=("arbitrary", "arbitrary"), vmem_limit_bytes=VMEM_LIMIT),
        name="merge",
    )(y_gla, y_gdn, wb0, wb1, u, u)


def _outnorm_kernel(m_ref, w_ref, x_ref, gate_ref, lg_ref, lb_ref, o_ref, *, alpha):
    out = _dot(m_ref[...], w_ref[...])
    r = alpha * x_ref[...] + gate_ref[0] * out
    mu = jnp.mean(r, axis=-1, keepdims=True)
    rc = r - mu
    var = jnp.mean(rc * rc, axis=-1, keepdims=True)
    o_ref[...] = rc * lax.rsqrt(var + LN_EPS) * lg_ref[...] + lb_ref[...]


def _outnorm(merged, w_out, x2, ada3, ln_g, ln_b, seq, alpha):
    t, d = x2.shape
    tm = 512
    per_b = seq // tm
    return pl.pallas_call(
        functools.partial(_outnorm_kernel, alpha=alpha),
        grid=(t // tm,),
        in_specs=[
            pl.BlockSpec((tm, d), lambda m: (m, 0)),
            pl.BlockSpec((d, d), lambda m: (0, 0)),
            pl.BlockSpec((tm, d), lambda m: (m, 0)),
            pl.BlockSpec((1, 1, d), lambda m: (m // per_b, 0, 2)),
            pl.BlockSpec((1, d), lambda m: (0, 0)),
            pl.BlockSpec((1, d), lambda m: (0, 0)),
        ],
        out_specs=pl.BlockSpec((tm, d), lambda m: (m, 0)),
        out_shape=jax.ShapeDtypeStruct((t, d), F32),
        compiler_params=pltpu.CompilerParams(
            dimension_semantics=("arbitrary",), vmem_limit_bytes=VMEM_LIMIT),
        name="outnorm",
    )(merged, w_out, x2, ada3, ln_g, ln_b)


def _pack_w_in(w_in):
    gq, gk, gv, gg, glr = 0, 1024, 2048, 4096, 6144
    dq = 6160
    dk_, dv_, dz_, db_, da_ = dq + 2048, dq + 4096, dq + 6144, dq + 8192, dq + 8208
    mg = dq + 8224
    main = jnp.concatenate([w_in[:, gq:glr], w_in[:, dq:db_], w_in[:, mg:mg + 2 * D_MODEL]], axis=1)
    pad = jnp.zeros((w_in.shape[0], N_SMALL - 3 * GLA_RANK), w_in.dtype)
    small = jnp.concatenate([w_in[:, glr:glr + GLA_RANK], w_in[:, db_:db_ + GDN_HEADS],
                             w_in[:, da_:da_ + GDN_HEADS], pad], axis=1)
    return main.astype(BF16), small.astype(BF16)


def _layer(x, c, w_ada, b_ada, w_in, w_gk2, b_gk2, conv_w, a_log, dt_bias,
           gla_norm_w, gdn_norm_w, w_branch, w_out, ln_g, ln_b, alpha):
    bsz, seq, d = x.shape
    t = bsz * seq
    x2 = x.reshape(t, d)

    ada = _ada(c, w_ada, b_ada)
    ada3 = ada.reshape(bsz, 1, 3 * d)

    w_main, w_small = _pack_w_in(w_in)
    u, small = _inproj(x2, ada3, w_main, w_small, seq)

    w2p = jnp.concatenate(
        [w_gk2, jnp.zeros((N_SMALL - GLA_RANK, w_gk2.shape[1]), w_gk2.dtype)], axis=0).astype(BF16)
    y_gla = _gla(u, small, w2p, b_gk2.reshape(1, -1), gla_norm_w.reshape(1, -1), bsz, seq)

    cw8 = jnp.concatenate([conv_w, jnp.zeros((8 - CONV_K, conv_w.shape[1]), conv_w.dtype)], axis=0)
    lane_pad = lambda v: jnp.zeros((1, N_SMALL), F32).at[0, LANE_A:LANE_A + GDN_HEADS].set(v)
    y_gdn = _gdn(u, small, cw8, lane_pad(a_log), lane_pad(dt_bias), gdn_norm_w.reshape(1, -1), bsz, seq)

    merged = _merge(y_gla, y_gdn, w_branch[0].astype(BF16), w_branch[1].astype(BF16), u)
    out = _outnorm(merged, w_out.astype(BF16), x2, ada3, ln_g.reshape(1, -1), ln_b.reshape(1, -1),
                   seq, alpha)
    return out.reshape(bsz, seq, d)


def kernel(x, c, w_ada, b_ada, w_in, w_gk2, b_gk2, conv_w, a_log, dt_bias, gla_norm_w, gdn_norm_w,
           w_branch, w_out, ln_g, ln_b):
    depth = w_ada.shape[0]
    alpha = (2 * depth) ** 0.25
    for l in range(depth):
        x = _layer(x, c, w_ada[l], b_ada[l], w_in[l], w_gk2[l], b_gk2[l], conv_w[l], a_log[l],
                   dt_bias[l], gla_norm_w[l], gdn_norm_w[l], w_branch[l], w_out[l], ln_g[l], ln_b[l],
                   alpha)
    return x
```

```python
import functools

import jax
import jax.numpy as jnp
from jax import lax
from jax.experimental import pallas as pl
from jax.experimental.pallas import tpu as pltpu

F32 = jnp.float32
BF16 = jnp.bfloat16

D_MODEL = 2048
GLA_HEADS = 4
GLA_DK = 256
GLA_DV = 512
GLA_RANK = 16
GLA_GATE_NORM = 16.0
GDN_HEADS = 16
GDN_DK = 128
GDN_DV = 128
CONV_K = 4
NORM_EPS = 1e-6
LN_EPS = 1e-5

GLA_CHUNK = 64
GDN_CHUNK = 128

OFF_GLA_Q = 0
OFF_GLA_K = 1024
OFF_GLA_V = 2048
OFF_GLA_G = 4096
OFF_GDN_Q = 6144
OFF_GDN_K = 8192
OFF_GDN_V = 10240
OFF_GDN_Z = 12288
OFF_MERGE = 14336
N_MAIN = 18432
N_SMALL = 128
LANE_B = 16
LANE_A = 32

VMEM_LIMIT = 56 * 1024 * 1024


def _sigmoid(x):
    return 1.0 / (1.0 + jnp.exp(-x))


def _softplus(x):
    return jnp.maximum(x, 0.0) + jnp.log1p(jnp.exp(-jnp.abs(x)))


def _dot(a, b):
    return jnp.dot(a, b, preferred_element_type=F32)


def _dot_nt(a, b):
    return lax.dot_general(a, b, (((1,), (1,)), ((), ())), preferred_element_type=F32)


def _dot_tn(a, b):
    return lax.dot_general(a, b, (((0,), (0,)), ((), ())), preferred_element_type=F32)


def _same_chunk(row, col, chunk):
    shift = chunk.bit_length() - 1
    return jnp.right_shift(row, shift) == jnp.right_shift(col, shift)


def _split3(x):
    hi = x.astype(BF16)
    r1 = x - hi.astype(F32)
    lo = r1.astype(BF16)
    lo2 = (r1 - lo.astype(F32)).astype(BF16)
    return hi, lo, lo2


def _dot_exact_lhs(m_bf16, x):
    hi, lo, lo2 = _split3(x)
    return _dot(m_bf16, hi) + _dot(m_bf16, lo) + _dot(m_bf16, lo2)


def _dot_exact_rhs(x, m_bf16):
    hi, lo, lo2 = _split3(x)
    return _dot(hi, m_bf16) + _dot(lo, m_bf16) + _dot(lo2, m_bf16)


def _ada_kernel(c_ref, w_ref, b_ref, o_ref):
    c = c_ref[...]
    s = (c * _sigmoid(c)).astype(BF16)
    o_ref[...] = _dot(s, w_ref[...].astype(BF16)) + b_ref[...]


def _ada(c, w_ada, b_ada):
    bsz, d = c.shape
    n = w_ada.shape[1]
    tn = 768
    return pl.pallas_call(
        _ada_kernel,
        grid=(n // tn,),
        in_specs=[
            pl.BlockSpec((bsz, d), lambda j: (0, 0)),
            pl.BlockSpec((d, tn), lambda j: (0, j)),
            pl.BlockSpec((1, tn), lambda j: (0, j)),
        ],
        out_specs=pl.BlockSpec((bsz, tn), lambda j: (0, j)),
        out_shape=jax.ShapeDtypeStruct((bsz, n), F32),
        compiler_params=pltpu.CompilerParams(
            dimension_semantics=("arbitrary",), vmem_limit_bytes=VMEM_LIMIT),
        name="ada",
    )(c, w_ada, b_ada.reshape(1, n))


def _inproj_kernel(x_ref, sc_ref, sh_ref, w_ref, ws_ref, u_ref, sm_ref, h_scr, *, tm):
    @pl.when(pl.program_id(1) == 0)
    def _():
        sc = 1.0 + sc_ref[0]
        sh = sh_ref[0]
        rb = 256
        for r in range(tm // rb):
            h = x_ref[r * rb:(r + 1) * rb, :] * sc + sh
            h_scr[r * rb:(r + 1) * rb, :] = h.astype(BF16)
        sm_ref[...] = _dot(h_scr[...], ws_ref[...])

    u_ref[...] = _dot(h_scr[...], w_ref[...]).astype(u_ref.dtype)


def _inproj(x2, ada3, w_main, w_small, seq):
    t, d = x2.shape
    tm, tn = 1024, 1024
    per_b = seq // tm
    return pl.pallas_call(
        functools.partial(_inproj_kernel, tm=tm),
        grid=(t // tm, N_MAIN // tn),
        in_specs=[
            pl.BlockSpec((tm, d), lambda m, n: (m, 0)),
            pl.BlockSpec((1, 1, d), lambda m, n: (m // per_b, 0, 1)),
            pl.BlockSpec((1, 1, d), lambda m, n: (m // per_b, 0, 0)),
            pl.BlockSpec((d, tn), lambda m, n: (0, n)),
            pl.BlockSpec((d, N_SMALL), lambda m, n: (0, 0)),
        ],
        out_specs=[
            pl.BlockSpec((tm, tn), lambda m, n: (m, n)),
            pl.BlockSpec((tm, N_SMALL), lambda m, n: (m, 0)),
        ],
        out_shape=[
            jax.ShapeDtypeStruct((t, N_MAIN), BF16),
            jax.ShapeDtypeStruct((t, N_SMALL), F32),
        ],
        scratch_shapes=[pltpu.VMEM((tm, d), BF16)],
        compiler_params=pltpu.CompilerParams(
            dimension_semantics=("arbitrary", "arbitrary"), vmem_limit_bytes=VMEM_LIMIT),
        name="inproj",
    )(x2, ada3, ada3, w_main, w_small)


def _gla_kernel(q_ref, k_ref, v_ref, g_ref, sm_ref, w2_ref, b2_ref, nw_ref, y_ref,
                qe_s, ki_s, kd_s, dl_s, st_s, *, seq):
    c = GLA_CHUNK
    rb = 256
    row = lax.broadcasted_iota(jnp.int32, (rb, rb), 0)
    col = lax.broadcasted_iota(jnp.int32, (rb, rb), 1)
    same = _same_chunk(row, col, c)
    l_incl = jnp.where(same & (col <= row), 1.0, 0.0).astype(BF16)
    l_full = jnp.where(same, 1.0, 0.0).astype(BF16)
    scale = GLA_DK ** -0.5

    def prep(i, carry):
        r0 = pl.multiple_of(i * rb, rb)
        rows = pl.ds(r0, rb)
        z = _dot(sm_ref[rows, :].astype(BF16), w2_ref[...]) + b2_ref[...]
        la = -_softplus(-z) * (1.0 / GLA_GATE_NORM)
        hi, lo, lo2 = _split3(la)
        cum = _dot(l_incl, hi) + _dot(l_incl, lo) + _dot(l_incl, lo2)
        last = _dot(l_full, hi) + _dot(l_full, lo) + _dot(l_full, lo2)
        q = q_ref[rows, :].astype(F32)
        k = k_ref[rows, :].astype(F32)
        qe_s[rows, :] = (q * (jnp.exp(cum) * scale)).astype(BF16)
        ki_s[rows, :] = (k * jnp.exp(-cum)).astype(BF16)
        kd_s[rows, :] = (k * jnp.exp(last - cum)).astype(BF16)
        dl_s[rows, :] = jnp.exp(last)
        return carry

    lax.fori_loop(0, seq // rb, prep, 0)

    st_s[...] = jnp.zeros_like(st_s)
    tr = lax.broadcasted_iota(jnp.int32, (c, c), 0)
    tc = lax.broadcasted_iota(jnp.int32, (c, c), 1)
    causal = tc <= tr

    def step(n, carry):
        r0 = pl.multiple_of(n * c, c)
        rows = pl.ds(r0, c)
        qe = qe_s[rows, :]
        ki = ki_s[rows, :]
        kd = kd_s[rows, :]
        v = v_ref[rows, :].astype(BF16)
        attn = jnp.where(causal, _dot_nt(qe, ki), 0.0).astype(BF16)
        st = st_s[...]
        o = _dot_nt(qe, st.astype(BF16)) + _dot(attn, v)
        dec = dl_s[pl.ds(r0, 8), :][0:1, :]
        st_s[...] = st * dec + _dot_tn(v, kd)
        ms = jnp.mean(o * o, axis=-1, keepdims=True)
        g = g_ref[rows, :].astype(F32)
        y = o * lax.rsqrt(ms + NORM_EPS) * nw_ref[...] * (g * _sigmoid(g))
        y_ref[rows, :] = y.astype(y_ref.dtype)
        return carry

    lax.fori_loop(0, seq // c, step, 0)


def _gla(u, small, w2p, b2, norm_w, bsz, seq):
    t = u.shape[0]
    qb, vb = GLA_DK, GLA_DV
    return pl.pallas_call(
        functools.partial(_gla_kernel, seq=seq),
        grid=(bsz, GLA_HEADS),
        in_specs=[
            pl.BlockSpec((seq, qb), lambda b, h: (b, OFF_GLA_Q // qb + h)),
            pl.BlockSpec((seq, qb), lambda b, h: (b, OFF_GLA_K // qb + h)),
            pl.BlockSpec((seq, vb), lambda b, h: (b, OFF_GLA_V // vb + h)),
            pl.BlockSpec((seq, vb), lambda b, h: (b, OFF_GLA_G // vb + h)),
            pl.BlockSpec((seq, N_SMALL), lambda b, h: (b, 0)),
            pl.BlockSpec((N_SMALL, qb), lambda b, h: (0, h)),
            pl.BlockSpec((1, qb), lambda b, h: (0, h)),
            pl.BlockSpec((1, vb), lambda b, h: (0, 0)),
        ],
        out_specs=pl.BlockSpec((seq, vb), lambda b, h: (b, h)),
        out_shape=jax.ShapeDtypeStruct((t, GLA_HEADS * GLA_DV), BF16),
        scratch_shapes=[
            pltpu.VMEM((seq, qb), BF16),
            pltpu.VMEM((seq, qb), BF16),
            pltpu.VMEM((seq, qb), BF16),
            pltpu.VMEM((seq, qb), F32),
            pltpu.VMEM((vb, qb), F32),
        ],
        compiler_params=pltpu.CompilerParams(
            dimension_semantics=("arbitrary", "arbitrary"), vmem_limit_bytes=VMEM_LIMIT),
        name="gla",
    )(u, u, u, u, small, w2p, b2, norm_w)


INV_BASE = 8
GDN_GROUP = 16


def _unit_lower_inverse(ms, eye, tr, tc):
    n = ms[0].shape[0]
    base = _same_chunk(tr, tc, INV_BASE)
    ps = [jnp.where(base, -m, 0.0) for m in ms]
    tinvs = [eye + p for p in ps]
    size = 2
    while size < INV_BASE:
        pbs = [p.astype(BF16) for p in ps]
        ps = [_dot(pb, pb) for pb in pbs]
        tinvs = [t + _dot(t.astype(BF16), p.astype(BF16)) for t, p in zip(tinvs, ps)]
        size *= 2
    blk = INV_BASE
    while blk < n:
        off = _same_chunk(tr, tc, 2 * blk) & jnp.logical_not(_same_chunk(tr, tc, blk))
        moffs = [jnp.where(off, m, 0.0).astype(BF16) for m in ms]
        tbs = [t.astype(BF16) for t in tinvs]
        mids = [_dot(tb, mo).astype(BF16) for tb, mo in zip(tbs, moffs)]
        tinvs = [t - _dot(mid, tb) for t, mid, tb in zip(tinvs, mids, tbs)]
        blk *= 2
    return tinvs


def _gdn_kernel(q_ref, k_ref, v_ref, z_ref, sm_ref, cwq_ref, cwk_ref, cwv_ref, al_ref, dt_ref,
                nw_ref, y_ref,
                xq_s, xk_s, xv_s, q_s, k_s, v_s, bc_s, cc_s,
                u_s, w_s, a_s, qd_s, kdt_s, gl_s, st_s, *, seq):
    c = GDN_CHUNK
    nchunk = seq // c
    h = pl.program_id(1)
    rb = 256
    scale = GDN_DK ** -0.5

    for src, dst in ((q_ref, xq_s), (k_ref, xk_s), (v_ref, xv_s)):
        dst[0:8, :] = jnp.zeros((8, GDN_DK), F32)
        dst[8:8 + seq, :] = src[...].astype(F32)

    row = lax.broadcasted_iota(jnp.int32, (rb, rb), 0)
    col = lax.broadcasted_iota(jnp.int32, (rb, rb), 1)
    l_incl = jnp.where(_same_chunk(row, col, c) & (col <= row), 1.0, 0.0).astype(BF16)
    er =lax.broadcasted_iota(jnp.int32, (N_SMALL, GDN_DK), 0)
    sel_b = jnp.where(er == LANE_B + h, 1.0, 0.0).astype(BF16)
    sel_a = jnp.where(er == LANE_A + h, 1.0, 0.0).astype(BF16)
    neg_a = -jnp.exp(al_ref[...])
    dtb = dt_ref[...]

    def conv_silu(xs, cw_ref, r0):
        acc = None
        for j in range(CONV_K):
            tap = xs[r0 + 8 - (CONV_K - 1) + j: r0 + 8 - (CONV_K - 1) + j + rb, :] * cw_ref[j:j + 1, :]
            acc = tap if acc is None else acc + tap
        return acc * _sigmoid(acc)

    def l2n(x):
        return x * lax.rsqrt(jnp.sum(x * x, axis=-1, keepdims=True) + NORM_EPS)

    for i in range(seq // rb):
        r0 = i * rb
        rows = slice(r0, r0 + rb)
        q_s[rows, :] = l2n(conv_silu(xq_s, cwq_ref, r0)) * scale
        k_s[rows, :] = l2n(conv_silu(xk_s, cwk_ref, r0))
        v_s[rows, :] = conv_silu(xv_s, cwv_ref, r0)
        sm = sm_ref[rows, :]
        beta_all = _sigmoid(sm)
        g_all = neg_a * _softplus(sm + dtb)
        bc_s[rows, :] = _dot_exact_rhs(beta_all, sel_b)
        gcol = _dot_exact_rhs(g_all, sel_a)
        cc_s[rows, :] = _dot_exact_lhs(l_incl, gcol)

    tr = lax.broadcasted_iota(jnp.int32, (c, c), 0)
    tc = lax.broadcasted_iota(jnp.int32, (c, c), 1)
    lower = tc <= tr
    strict = tc < tr
    eye = jnp.where(tc == tr, 1.0, 0.0).astype(F32)

    grp = GDN_GROUP

    def chunk_prep(i, carry):
        ns = [i * grp + j for j in range(grp)]
        rws = [pl.ds(pl.multiple_of(n * c, c), c) for n in ns]
        qs = [q_s[r, :] for r in rws]
        ks = [k_s[r, :] for r in rws]
        vs = [v_s[r, :].astype(BF16) for r in rws]
        bcols = [bc_s[r, :] for r in rws]
        ccols = [cc_s[r, :] for r in rws]
        brows = [b.T for b in bcols]
        crows = [cc.T for cc in ccols]
        gams = [jnp.where(lower, jnp.exp(jnp.where(lower, cc - cr, 0.0)), 0.0)
                for cc, cr in zip(ccols, crows)]
        kbs = [k.astype(BF16) for k in ks]
        kks = [_dot_nt(kb, kb) for kb in kbs]
        qks = [_dot_nt(q.astype(BF16), kb) for q, kb in zip(qs, kbs)]
        ms = [jnp.where(strict, b * kk * g, 0.0) for b, kk, g in zip(bcols, kks, gams)]
        tinvs = _unit_lower_inverse(ms, eye, tr, tc)
        tbs = [t * b for t, b in zip(tinvs, brows)]
        us = [_dot(tb.astype(BF16), v) for tb, v in zip(tbs, vs)]
        ws = [_dot((tb * jnp.exp(cr)).astype(BF16), kb) for tb, cr, kb in zip(tbs, crows, kbs)]
        for j, n in enumerate(ns):
            u_s[n] = us[j]
            w_s[n] = ws[j].astype(BF16)
            a_s[n] = (qks[j] * gams[j]).astype(BF16)
            qd_s[n] = (qs[j] * jnp.exp(ccols[j])).astype(BF16)
            lastrow = ccols[j][c - 1:c, :]
            kd = ks[j] * jnp.exp(lastrow - ccols[j])
            kdt_s[n] = kd.T.astype(BF16)
            gl_s[n] = jnp.broadcast_to(jnp.exp(lastrow), (8, GDN_DV))
        return carry

    lax.fori_loop(0, nchunk // grp, chunk_prep, 0)

    st_s[...] = jnp.zeros_like(st_s)

    def step(n, carry):
        r0 = pl.multiple_of(n * c, c)
        rows = pl.ds(r0, c)
        st = st_s[...]
        sb = st.astype(BF16)
        v_new = u_s[n] - _dot(w_s[n], sb)
        vb = v_new.astype(BF16)
        o = _dot(qd_s[n], sb) + _dot(a_s[n], vb)
        st_s[...] = st * gl_s[n][0:1, :] + _dot(kdt_s[n], vb)
        ms = jnp.mean(o * o, axis=-1, keepdims=True)
        z = z_ref[rows, :].astype(F32)
        y = o * lax.rsqrt(ms + NORM_EPS) * nw_ref[...] * (z * _sigmoid(z))
        y_ref[rows, :] = y.astype(y_ref.dtype)
        return carry

    lax.fori_loop(0, nchunk, step, 0, unroll=2)


def _gdn(u, small, cw8, alog_row, dt_row, norm_w, bsz, seq):
    t = u.shape[0]
    dk = GDN_DK
    nchunk = seq // GDN_CHUNK
    c = GDN_CHUNK
    return pl.pallas_call(
        functools.partial(_gdn_kernel, seq=seq),
        grid=(bsz, GDN_HEADS),
        in_specs=[
            pl.BlockSpec((seq, dk), lambda b, h: (b, OFF_GDN_Q // dk + h)),
            pl.BlockSpec((seq, dk), lambda b, h: (b, OFF_GDN_K // dk + h)),
            pl.BlockSpec((seq, dk), lambda b, h: (b, OFF_GDN_V // dk + h)),
            pl.BlockSpec((seq, dk), lambda b, h: (b, OFF_GDN_Z // dk + h)),
            pl.BlockSpec((seq, N_SMALL), lambda b, h: (b, 0)),
            pl.BlockSpec((8, dk), lambda b, h: (0, h)),
            pl.BlockSpec((8, dk), lambda b, h: (0, GDN_HEADS + h)),
            pl.BlockSpec((8, dk), lambda b, h: (0, 2 * GDN_HEADS + h)),
            pl.BlockSpec((1, N_SMALL), lambda b, h: (0, 0)),
            pl.BlockSpec((1, N_SMALL), lambda b, h: (0, 0)),
            pl.BlockSpec((1, dk), lambda b, h: (0, 0)),
        ],
        out_specs=pl.BlockSpec((seq, dk), lambda b, h: (b, h)),
        out_shape=jax.ShapeDtypeStruct((t, GDN_HEADS * GDN_DV), BF16),
        scratch_shapes=[
            pltpu.VMEM((seq + 8, dk), F32),
            pltpu.VMEM((seq + 8, dk), F32),
            pltpu.VMEM((seq + 8, dk), F32),
            pltpu.VMEM((seq, dk), F32),
            pltpu.VMEM((seq, dk), F32),
            pltpu.VMEM((seq, dk), F32),
            pltpu.VMEM((seq, dk), F32),
            pltpu.VMEM((seq, dk), F32),
            pltpu.VMEM((nchunk, c, dk), F32),
            pltpu.VMEM((nchunk, c, dk), BF16),
            pltpu.VMEM((nchunk, c, c), BF16),
            pltpu.VMEM((nchunk, c, dk), BF16),
            pltpu.VMEM((nchunk, dk, c), BF16),
            pltpu.VMEM((nchunk, 8, dk), F32),
            pltpu.VMEM((dk, dk), F32),
        ],
        compiler_params=pltpu.CompilerParams(
            dimension_semantics=("arbitrary", "arbitrary"), vmem_limit_bytes=VMEM_LIMIT),
        name="gdn",
    )(u, u, u, u, small, cw8, cw8, cw8, alog_row, dt_row, norm_w)


def _merge_kernel(yg_ref, yd_ref, wa_ref, wb_ref, ga_ref, gb_ref, o_ref):
    pa = _dot(yg_ref[...], wa_ref[...])
    pb = _dot(yd_ref[...], wb_ref[...])
    ga = _sigmoid(ga_ref[...].astype(F32))
    gb = _sigmoid(gb_ref[...].astype(F32))
    o_ref[...] = (ga * pa + gb * pb).astype(o_ref.dtype)


def _merge(y_gla, y_gdn, wb0, wb1, u):
    t, d = y_gla.shape
    tm, tn = 1024, 512
    return pl.pallas_call(
        _merge_kernel,
        grid=(t // tm, d // tn),
        in_specs=[
            pl.BlockSpec((tm, d), lambda m, n: (m, 0)),
            pl.BlockSpec((tm, d), lambda m, n: (m, 0)),
            pl.BlockSpec((d, tn), lambda m, n: (0, n)),
            pl.BlockSpec((d, tn), lambda m, n: (0, n)),
            pl.BlockSpec((tm, tn), lambda m, n: (m, OFF_MERGE // tn + n)),
            pl.BlockSpec((tm, tn), lambda m, n: (m, (OFF_MERGE + D_MODEL) // tn + n)),
        ],
        out_specs=pl.BlockSpec((tm, tn), lambda m, n: (m, n)),
        out_shape=jax.ShapeDtypeStruct((t, d), BF16),
        compiler_params=pltpu.CompilerParams(
            dimension_semantics=("arbitrary", "arbitrary"), vmem_limit_bytes=VMEM_LIMIT),
        name="merge",
    )(y_gla, y_gdn, wb0, wb1, u, u)


def _outnorm_kernel(m_ref, w_ref, x_ref, gate_ref, lg_ref, lb_ref, o_ref, *, alpha):
    out = _dot(m_ref[...], w_ref[...])
    r = alpha * x_ref[...] + gate_ref[0] * out
    mu = jnp.mean(r, axis=-1, keepdims=True)
    rc = r - mu
    var = jnp.mean(rc * rc, axis=-1, keepdims=True)
    o_ref[...] = rc * lax.rsqrt(var + LN_EPS) * lg_ref[...] + lb_ref[...]


def _outnorm(merged, w_out, x2, ada3, ln_g, ln_b, seq, alpha):
    t, d = x2.shape
    tm = 512
    per_b = seq // tm
    return pl.pallas_call(
        functools.partial(_outnorm_kernel, alpha=alpha),
        grid=(t // tm,),
        in_specs=[
            pl.BlockSpec((tm, d), lambda m: (m, 0)),
            pl.BlockSpec((d, d), lambda m: (0, 0)),
            pl.BlockSpec((tm, d), lambda m: (m, 0)),
            pl.BlockSpec((1, 1, d), lambda m: (m // per_b, 0, 2)),
            pl.BlockSpec((1, d), lambda m: (0, 0)),
            pl.BlockSpec((1, d), lambda m: (0, 0)),
        ],
        out_specs=pl.BlockSpec((tm, d), lambda m: (m, 0)),
        out_shape=jax.ShapeDtypeStruct((t, d), F32),
        compiler_params=pltpu.CompilerParams(
            dimension_semantics=("arbitrary",), vmem_limit_bytes=VMEM_LIMIT),
        name="outnorm",
    )(merged, w_out, x2, ada3, ln_g, ln_b)


def _pack_w_in(w_in):
    gq, gk, gv, gg, glr = 0, 1024, 2048, 4096, 6144
    dq = 6160
    dk_, dv_, dz_, db_, da_ = dq + 2048, dq + 4096, dq + 6144, dq + 8192, dq + 8208
    mg = dq + 8224
    main = jnp.concatenate([w_in[:, gq:glr], w_in[:, dq:db_], w_in[:, mg:mg + 2 * D_MODEL]], axis=1)
    pad = jnp.zeros((w_in.shape[0], N_SMALL - 3 * GLA_RANK), w_in.dtype)
    small = jnp.concatenate([w_in[:, glr:glr + GLA_RANK], w_in[:, db_:db_ + GDN_HEADS],
                             w_in[:, da_:da_ + GDN_HEADS], pad], axis=1)
    return main.astype(BF16), small.astype(BF16)


def _layer(x, c, w_ada, b_ada, w_in, w_gk2, b_gk2, conv_w, a_log, dt_bias,
           gla_norm_w, gdn_norm_w, w_branch, w_out, ln_g, ln_b, alpha):
    bsz, seq, d = x.shape
    t = bsz * seq
    x2 = x.reshape(t, d)

    ada = _ada(c, w_ada, b_ada)
    ada3 = ada.reshape(bsz, 1, 3 * d)

    w_main, w_small = _pack_w_in(w_in)
    u, small = _inproj(x2, ada3, w_main, w_small, seq)

    w2p = jnp.concatenate(
        [w_gk2, jnp.zeros((N_SMALL - GLA_RANK, w_gk2.shape[1]), w_gk2.dtype)], axis=0).astype(BF16)
    y_gla = _gla(u, small, w2p, b_gk2.reshape(1, -1), gla_norm_w.reshape(1, -1), bsz, seq)

    cw8 = jnp.concatenate([conv_w, jnp.zeros((8 - CONV_K, conv_w.shape[1]), conv_w.dtype)], axis=0)
    lane_pad = lambda v: jnp.zeros((1, N_SMALL), F32).at[0, LANE_A:LANE_A + GDN_HEADS].set(v)
    y_gdn = _gdn(u, small, cw8, lane_pad(a_log), lane_pad(dt_bias), gdn_norm_w.reshape(1, -1), bsz, seq)

    merged = _merge(y_gla, y_gdn, w_branch[0].astype(BF16), w_branch[1].astype(BF16), u)
    out = _outnorm(merged, w_out.astype(BF16), x2, ada3, ln_g.reshape(1, -1), ln_b.reshape(1, -1),
                   seq, alpha)
    return out.reshape(bsz, seq, d)


def kernel(x, c, w_ada, b_ada, w_in, w_gk2, b_gk2, conv_w, a_log, dt_bias, gla_norm_w, gdn_norm_w,
           w_branch, w_out, ln_g, ln_b):
    depth = w_ada.shape[0]
    alpha = (2 * depth) ** 0.25
    for l in range(depth):
        x = _layer(x, c, w_ada[l], b_ada[l], w_in[l], w_gk2[l], b_gk2[l], conv_w[l], a_log[l],
                   dt_bias[l], gla_norm_w[l], gdn_norm_w[l], w_branch[l], w_out[l], ln_g[l], ln_b[l],
                   alpha)
    return x
```

```python
import functools

import jax
import jax.numpy as jnp
from jax import lax
from jax.experimental import pallas as pl
from jax.experimental.pallas import tpu as pltpu

F32 = jnp.float32
BF16 = jnp.bfloat16

D_MODEL = 2048
GLA_HEADS = 4
GLA_DK = 256
GLA_DV = 512
GLA_RANK = 16
GLA_GATE_NORM = 16.0
GDN_HEADS = 16
GDN_DK = 128
GDN_DV = 128
CONV_K = 4
NORM_EPS = 1e-6
LN_EPS = 1e-5

GLA_CHUNK = 64
GLA_GROUP = 8
GDN_CHUNK = 128
GDN_PREP_GROUP = 16
GDN_HEADS_PER_STEP = 2
INV_BASE = 8

OFF_GLA_Q = 0
OFF_GLA_K = 1024
OFF_GLA_V = 2048
OFF_GLA_G = 4096
OFF_GDN_Q = 6144
OFF_GDN_K = 8192
OFF_GDN_V = 10240
OFF_GDN_Z = 12288
OFF_MERGE = 14336
N_MAIN = 18432
N_SMALL = 128
LANE_B = 16
LANE_A = 32

ROW_BLOCK = 256
VMEM_LIMIT = 56 * 1024 * 1024


def _sigmoid(x):
    return 0.5 + 0.5 * jnp.tanh(0.5 * x)


def _silu(x):
    hx = 0.5 * x
    return hx + hx * jnp.tanh(hx)


def _softplus(x):
    return jnp.maximum(x, 0.0) + jnp.log1p(jnp.exp(-jnp.abs(x)))


def _dot(a, b):
    return jnp.dot(a, b, preferred_element_type=F32)


def _dot_nt(a, b):
    return lax.dot_general(a, b, (((1,), (1,)), ((), ())), preferred_element_type=F32)


def _dot_tn(a, b):
    return lax.dot_general(a, b, (((0,), (0,)), ((), ())), preferred_element_type=F32)


def _same_chunk(row, col, chunk):
    shift = chunk.bit_length() - 1
    return jnp.right_shift(row, shift) == jnp.right_shift(col, shift)


def _cumsum_matrix(n, chunk):
    row = lax.broadcasted_iota(jnp.int32, (n, n), 0)
    col = lax.broadcasted_iota(jnp.int32, (n, n), 1)
    return jnp.where(_same_chunk(row, col, chunk) & (col <= row), 1.0, 0.0).astype(BF16)


def _split3(x):
    hi = x.astype(BF16)
    r1 = x - hi.astype(F32)
    lo = r1.astype(BF16)
    lo2 = (r1 - lo.astype(F32)).astype(BF16)
    return hi, lo, lo2


def _dot_exact_lhs(m_bf16, x):
    hi, lo, lo2 = _split3(x)
    return _dot(m_bf16, hi) + _dot(m_bf16, lo) + _dot(m_bf16, lo2)


def _dot_exact_rhs(x, m_bf16):
    hi, lo, lo2 = _split3(x)
    return _dot(hi, m_bf16) + _dot(lo, m_bf16) + _dot(lo2, m_bf16)


def _gated_rmsnorm(o, norm_w, gate):
    ms = jnp.mean(o * o, axis=-1, keepdims=True)
    return o * lax.rsqrt(ms + NORM_EPS) * norm_w * _silu(gate)


def _ada_kernel(c_ref, w_ref, b_ref, o_ref):
    s = _silu(c_ref[...]).astype(BF16)
    o_ref[...] = _dot(s, w_ref[...].astype(BF16)) + b_ref[...]


def _ada(c, w_ada, b_ada):
    bsz, d = c.shape
    n = w_ada.shape[1]
    tn = 768
    return pl.pallas_call(
        _ada_kernel,
        grid=(n // tn,),
        in_specs=[
            pl.BlockSpec((bsz, d), lambda j: (0, 0)),
            pl.BlockSpec((d, tn), lambda j: (0, j)),
            pl.BlockSpec((1, tn), lambda j: (0, j)),
        ],
        out_specs=pl.BlockSpec((bsz, tn), lambda j: (0, j)),
        out_shape=jax.ShapeDtypeStruct((bsz, n), F32),
        compiler_params=pltpu.CompilerParams(
            dimension_semantics=("arbitrary",), vmem_limit_bytes=VMEM_LIMIT),
        name="ada",
    )(c, w_ada, b_ada.reshape(1, n))


def _inproj_kernel(x_ref, sc_ref, sh_ref, wa_ref, wb_ref, wc_ref, ws_ref, al_ref, dt_ref,
                   u_ref, sm_ref, h_scr, *, tm, nb_a, nb_b):
    n = pl.program_id(1)

    @pl.when(n == 0)
    def _():
        sc = 1.0 + sc_ref[0]
        sh = sh_ref[0]
        rb = ROW_BLOCK
        for r in range(tm // rb):
            h = x_ref[r * rb:(r + 1) * rb, :] * sc + sh
            h_scr[r * rb:(r + 1) * rb, :] = h.astype(BF16)
        l_incl = _cumsum_matrix(rb, GDN_CHUNK)
        neg_a = -jnp.exp(al_ref[...])
        lane = lax.broadcasted_iota(jnp.int32, (rb, N_SMALL), 1)
        for r in range(tm // rb):
            raw = _dot(h_scr[r * rb:(r + 1) * rb, :], ws_ref[...])
            g = neg_a * _softplus(raw + dt_ref[...])
            cum = _dot_exact_lhs(l_incl, g)
            sm_ref[r * rb:(r + 1) * rb, :] = jnp.where(
                lane < LANE_B, raw, jnp.where(lane < LANE_A, _sigmoid(raw), cum))

    @pl.when(n < nb_a)
    def _():
        u_ref[...] = _dot(h_scr[...], wa_ref[...]).astype(u_ref.dtype)

    @pl.when((n >= nb_a) & (n < nb_a + nb_b))
    def _():
        u_ref[...] = _dot(h_scr[...], wb_ref[...]).astype(u_ref.dtype)

    @pl.when(n >= nb_a + nb_b)
    def _():
        u_ref[...] = _dot(h_scr[...], wc_ref[...]).astype(u_ref.dtype)


def _inproj(x2, ada3, w_a, w_b, w_c, w_small, alog_row, dt_row, seq):
    t, d = x2.shape
    tm, tn = 1024, 1024
    per_b = seq // tm
    nb_a, nb_b, nb_c = w_a.shape[1] // tn, w_b.shape[1] // tn, w_c.shape[1] // tn
    assert (nb_a + nb_b + nb_c) * tn == N_MAIN
    return pl.pallas_call(
        functools.partial(_inproj_kernel, tm=tm, nb_a=nb_a, nb_b=nb_b),
        grid=(t // tm, N_MAIN // tn),
        in_specs=[
            pl.BlockSpec((tm, d), lambda m, n: (m, 0)),
            pl.BlockSpec((1, 1, d), lambda m, n: (m // per_b, 0, 1)),
            pl.BlockSpec((1, 1, d), lambda m, n: (m // per_b, 0, 0)),
            pl.BlockSpec((d, tn), lambda m, n: (0, jnp.minimum(n, nb_a - 1))),
            pl.BlockSpec((d, tn), lambda m, n: (0, jnp.clip(n - nb_a, 0, nb_b - 1))),
            pl.BlockSpec((d, tn), lambda m, n: (0, jnp.clip(n - nb_a - nb_b, 0, nb_c - 1))),
            pl.BlockSpec((d, N_SMALL), lambda m, n: (0, 0)),
            pl.BlockSpec((1, N_SMALL), lambda m, n: (0, 0)),
            pl.BlockSpec((1, N_SMALL), lambda m, n: (0, 0)),
        ],
        out_specs=[
            pl.BlockSpec((tm, tn), lambda m, n: (m, n)),
            pl.BlockSpec((tm, N_SMALL), lambda m, n: (m, 0)),
        ],
        out_shape=[
            jax.ShapeDtypeStruct((t, N_MAIN), BF16),
            jax.ShapeDtypeStruct((t, N_SMALL), F32),
        ],
        scratch_shapes=[pltpu.VMEM((tm, d), BF16)],
        compiler_params=pltpu.CompilerParams(
            dimension_semantics=("arbitrary", "arbitrary"), vmem_limit_bytes=VMEM_LIMIT),
        name="inproj",
    )(x2, ada3, ada3, w_a, w_b, w_c, w_small, alog_row, dt_row)


def _gla_kernel(q_ref, k_ref, v_ref, g_ref, sm_ref, w2_ref, b2_ref, nw_ref, y_ref,
                qe_s, ki_s, kd_s, dl_s, st_s, *, seq):
    c = GLA_CHUNK
    rb = ROW_BLOCK
    cpb = rb // c
    l_incl = _cumsum_matrix(rb, c)
    scale = GLA_DK ** -0.5

    def prep(i, carry):
        rows = pl.ds(pl.multiple_of(i * rb, rb), rb)
        z = _dot(sm_ref[rows, :].astype(BF16), w2_ref[...]) + b2_ref[...]
        la = -_softplus(-z) * (1.0 / GLA_GATE_NORM)
        cum = _dot_exact_lhs(l_incl, la)
        lasts = [cum[j * c + c - 1:j * c + c, :] for j in range(cpb)]
        last = jnp.concatenate([jnp.broadcast_to(l, (c, GLA_DK)) for l in lasts], axis=0)
        q = q_ref[rows, :].astype(F32)
        k = k_ref[rows, :].astype(F32)
        qe_s[rows, :] = (q * (jnp.exp(cum) * scale)).astype(BF16)
        ki_s[rows, :] = (k * jnp.exp(-cum)).astype(BF16)
        kd_s[rows, :] = (k * jnp.exp(last - cum)).astype(BF16)
        for j in range(cpb):
            dl_s[i * cpb + j] = jnp.broadcast_to(jnp.exp(lasts[j]), (8, GLA_DK))
        return carry

    lax.fori_loop(0, seq // rb, prep, 0)

    st_s[...] = jnp.zeros_like(st_s)
    tr = lax.broadcasted_iota(jnp.int32, (c, c), 0)
    tc = lax.broadcasted_iota(jnp.int32, (c, c), 1)
    causal = tc <= tr
    grp = GLA_GROUP

    def step(i, carry):
        ns = [i * grp + j for j in range(grp)]
        rws = [pl.ds(pl.multiple_of(n * c, c), c) for n in ns]
        qes = [qe_s[r, :] for r in rws]
        vs = [v_ref[r, :] for r in rws]
        attns = [jnp.where(causal, _dot_nt(qe, ki_s[r, :]), 0.0).astype(BF16) for qe, r in zip(qes, rws)]
        upds = [_dot_tn(v, kd_s[r, :]) for v, r in zip(vs, rws)]
        st = st_s[...]
        sts = []
        for j, n in enumerate(ns):
            sts.append(st.astype(BF16))
            st = st * dl_s[n][0:1, :] + upds[j]
        st_s[...] = st
        for j, r in enumerate(rws):
            o = _dot_nt(qes[j], sts[j]) + _dot(attns[j], vs[j])
            y = _gated_rmsnorm(o, nw_ref[...], g_ref[r, :].astype(F32))
            y_ref[r, :] = y.astype(y_ref.dtype)
        return carry

    lax.fori_loop(0, seq // (c * grp), step, 0)


def _gla(u, small, w2p, b2, norm_w, bsz, seq):
    t = u.shape[0]
    qb, vb = GLA_DK, GLA_DV
    return pl.pallas_call(
        functools.partial(_gla_kernel, seq=seq),
        grid=(bsz, GLA_HEADS),
        in_specs=[
            pl.BlockSpec((seq, qb), lambda b, h: (b, OFF_GLA_Q // qb + h)),
            pl.BlockSpec((seq, qb), lambda b, h: (b, OFF_GLA_K // qb + h)),
            pl.BlockSpec((seq, vb), lambda b, h: (b, OFF_GLA_V // vb + h)),
            pl.BlockSpec((seq, vb), lambda b, h: (b, OFF_GLA_G // vb + h)),
            pl.BlockSpec((seq, N_SMALL), lambda b, h: (b, 0)),
            pl.BlockSpec((N_SMALL, qb), lambda b, h: (0, h)),
            pl.BlockSpec((1, qb), lambda b, h: (0, h)),
            pl.BlockSpec((1, vb), lambda b, h: (0, 0)),
        ],
        out_specs=pl.BlockSpec((seq, vb), lambda b, h: (b, h)),
        out_shape=jax.ShapeDtypeStruct((t, GLA_HEADS * GLA_DV), BF16),
        scratch_shapes=[
            pltpu.VMEM((seq, qb), BF16),
            pltpu.VMEM((seq, qb), BF16),
            pltpu.VMEM((seq, qb), BF16),
            pltpu.VMEM((seq // GLA_CHUNK, 8, qb), F32),
            pltpu.VMEM((vb, qb), F32),
        ],
        compiler_params=pltpu.CompilerParams(
            dimension_semantics=("arbitrary", "arbitrary"), vmem_limit_bytes=VMEM_LIMIT),
        name="gla",
    )(u, u, u, u, small, w2p, b2, norm_w)


def _unit_lower_inverse(ms, eye, tr, tc):
    n = ms[0].shape[0]
    base = _same_chunk(tr, tc, INV_BASE)
    ps = [jnp.where(base, -m, 0.0) for m in ms]
    tinvs = [eye + p for p in ps]
    size = 2
    while size < INV_BASE:
        pbs = [p.astype(BF16) for p in ps]
        ps = [_dot(pb, pb) for pb in pbs]
        tinvs = [t + _dot(t.astype(BF16), p.astype(BF16)) for t, p in zip(tinvs, ps)]
        size *= 2
    blk = INV_BASE
    while blk < n:
        off = _same_chunk(tr, tc, 2 * blk) & jnp.logical_not(_same_chunk(tr, tc, blk))
        moffs = [jnp.where(off, m, 0.0).astype(BF16) for m in ms]
        tbs = [t.astype(BF16) for t in tinvs]
        mids = [_dot(tb, mo).astype(BF16) for tb, mo in zip(tbs, moffs)]
        tinvs = [t - _dot(mid, tb) for t, mid, tb in zip(tinvs, mids, tbs)]
        blk *= 2
    return tinvs


def _gdn_kernel(q_ref, k_ref, v_ref, z_ref, sm_ref, cwq_ref, cwk_ref, cwv_ref, nw_ref, y_ref,
                xq_s, xk_s, xv_s, q_s, k_s, v_s, bc_s, cc_s,
                u_s, wq_s, a_s, ws_s, bs_s, gl_s, *, seq):
    c = GDN_CHUNK
    nchunk = seq // c
    rb = ROW_BLOCK
    dk = GDN_DK
    scale = GDN_DK ** -0.5

    er = lax.broadcasted_iota(jnp.int32, (N_SMALL, 2 * dk), 0)
    ec = lax.broadcasted_iota(jnp.int32, (N_SMALL, 2 * dk), 1)
    sel_lane = jnp.where(ec < dk, LANE_B, LANE_A)
    tr = lax.broadcasted_iota(jnp.int32, (c, c), 0)
    tc = lax.broadcasted_iota(jnp.int32, (c, c), 1)
    lower = tc <= tr
    strict = tc < tr
    eye = jnp.where(tc == tr, 1.0, 0.0).astype(F32)

    def conv_silu(xs, cw_ref, lanes, r0):
        acc = None
        for j in range(CONV_K):
            lo = r0 + 8 - (CONV_K - 1) + j
            tap = xs[lo:lo + rb, :] * cw_ref[j:j + 1, lanes]
            acc = tap if acc is None else acc + tap
        return _silu(acc)

    def l2n(x):
        return x * lax.rsqrt(jnp.sum(x * x, axis=-1, keepdims=True) + NORM_EPS)

    for hp in range(GDN_HEADS_PER_STEP):
        lanes = slice(hp * dk, (hp + 1) * dk)
        head = pl.program_id(1) * GDN_HEADS_PER_STEP + hp

        for src, dst in ((q_ref, xq_s), (k_ref, xk_s), (v_ref, xv_s)):
            dst[0:8, :] = jnp.zeros((8, dk), F32)
            dst[8:8 + seq, :] = src[:, lanes].astype(F32)

        sel = jnp.where(er == sel_lane + head, 1.0, 0.0).astype(BF16)

        for i in range(seq // rb):
            r0 = i * rb
            rows = slice(r0, r0 + rb)
            q_s[rows, :] = l2n(conv_silu(xq_s, cwq_ref, lanes, r0)) * scale
            k_s[rows, :] = l2n(conv_silu(xk_s, cwk_ref, lanes, r0))
            v_s[rows, :] = conv_silu(xv_s, cwv_ref, lanes, r0)
            both = _dot_exact_rhs(sm_ref[rows, :], sel)
            bc_s[rows, :] = both[:, :dk]
            cc_s[rows, :] = both[:, dk:]

        for g0 in range(0, nchunk, GDN_PREP_GROUP):
            ns = list(range(g0, g0 + GDN_PREP_GROUP))
            rws = [slice(n * c, (n + 1) * c) for n in ns]
            qs = [q_s[r, :] for r in rws]
            ks = [k_s[r, :] for r in rws]
            vs = [v_s[r, :].astype(BF16) for r in rws]
            bcols = [bc_s[r, :] for r in rws]
            ccols = [cc_s[r, :] for r in rws]
            brows = [b.T for b in bcols]
            crows = [cc.T for cc in ccols]
            gams = [jnp.where(lower, jnp.exp(jnp.where(lower, cc - cr, 0.0)), 0.0)
                    for cc, cr in zip(ccols, crows)]
            kbs = [k.astype(BF16) for k in ks]
            kks = [_dot_nt(kb, kb) for kb in kbs]
            qks = [_dot_nt(q.astype(BF16), kb) for q, kb in zip(qs, kbs)]
            ms = [jnp.where(strict, b * kk * g, 0.0) for b, kk, g in zip(bcols, kks, gams)]
            tinvs = _unit_lower_inverse(ms, eye, tr, tc)
            tbs = [t * b for t, b in zip(tinvs, brows)]
            us = [_dot(tb.astype(BF16), v) for tb, v in zip(tbs, vs)]
            ws = [_dot((tb * jnp.exp(cr)).astype(BF16), kb).astype(BF16)
                  for tb, cr, kb in zip(tbs, crows, kbs)]
            lastrows = [cc[c - 1:c, :] for cc in ccols]
            kdts = [(k * jnp.exp(l - cc)).T.astype(BF16) for k, l, cc in zip(ks, lastrows, ccols)]
            wss = [_dot(kdt, w) for kdt, w in zip(kdts, ws)]
            bss = [_dot(kdt, u.astype(BF16)) for kdt, u in zip(kdts, us)]
            for j, n in enumerate(ns):
                u_s[hp, n] = us[j]
                wq_s[hp, n, 0:c, :] = ws[j]
                wq_s[hp, n, c:2 * c, :] = (qs[j] * jnp.exp(ccols[j])).astype(BF16)
                a_s[hp, n] = (qks[j] * gams[j]).astype(BF16)
                ws_s[hp, n] = wss[j].astype(BF16)
                bs_s[hp, n] = bss[j]
                gl_s[hp, n] = jnp.broadcast_to(jnp.exp(lastrows[j]), (8, GDN_DV))

    sts = [jnp.zeros((dk, GDN_DV), F32) for _ in range(GDN_HEADS_PER_STEP)]
    for n in range(nchunk):
        rows = slice(n * c, (n + 1) * c)
        for hp in range(GDN_HEADS_PER_STEP):
            lanes = slice(hp * dk, (hp + 1) * dk)
            sb = sts[hp].astype(BF16)
            sts[hp] = sts[hp] * gl_s[hp, n][0:1, :] - _dot(ws_s[hp, n], sb) + bs_s[hp, n]
            wq = _dot(wq_s[hp, n], sb)
            vb = (u_s[hp, n] - wq[0:c, :]).astype(BF16)
            o = wq[c:2 * c, :] + _dot(a_s[hp, n], vb)
            y = _gated_rmsnorm(o, nw_ref[...], z_ref[rows, lanes].astype(F32))
            y_ref[rows, lanes] = y.astype(y_ref.dtype)


def _gdn(u, small, cw8, norm_w, bsz, seq):
    t = u.shape[0]
    dk = GDN_DK
    hps = GDN_HEADS_PER_STEP
    wb = hps * dk
    nchunk = seq // GDN_CHUNK
    c = GDN_CHUNK
    conv_blocks = GDN_HEADS // hps
    return pl.pallas_call(
        functools.partial(_gdn_kernel, seq=seq),
        grid=(bsz, GDN_HEADS // hps),
        in_specs=[
            pl.BlockSpec((seq, wb), lambda b, p: (b, OFF_GDN_Q // wb + p)),
            pl.BlockSpec((seq, wb), lambda b, p: (b, OFF_GDN_K // wb + p)),
            pl.BlockSpec((seq, wb), lambda b, p: (b, OFF_GDN_V // wb + p)),
            pl.BlockSpec((seq, wb), lambda b, p: (b, OFF_GDN_Z // wb + p)),
            pl.BlockSpec((seq, N_SMALL), lambda b, p: (b, 0)),
            pl.BlockSpec((8, wb), lambda b, p: (0, p)),
            pl.BlockSpec((8, wb), lambda b, p: (0, conv_blocks + p)),
            pl.BlockSpec((8, wb), lambda b, p: (0, 2 * conv_blocks + p)),
            pl.BlockSpec((1, dk), lambda b, p: (0, 0)),
        ],
        out_specs=pl.BlockSpec((seq, wb), lambda b, p: (b, p)),
        out_shape=jax.ShapeDtypeStruct((t, GDN_HEADS * GDN_DV), BF16),
        scratch_shapes=[
            pltpu.VMEM((seq + 8, dk), F32),
            pltpu.VMEM((seq + 8, dk), F32),
            pltpu.VMEM((seq + 8, dk), F32),
            pltpu.VMEM((seq, dk), F32),
            pltpu.VMEM((seq, dk), F32),
            pltpu.VMEM((seq, dk), F32),
            pltpu.VMEM((seq, dk), F32),
            pltpu.VMEM((seq, dk), F32),
            pltpu.VMEM((hps, nchunk, c, dk), F32),
            pltpu.VMEM((hps, nchunk, 2 * c, dk), BF16),
            pltpu.VMEM((hps, nchunk, c, c), BF16),
            pltpu.VMEM((hps, nchunk, dk, dk), BF16),
            pltpu.VMEM((hps, nchunk, dk, dk), F32),
            pltpu.VMEM((hps, nchunk, 8, dk), F32),
        ],
        compiler_params=pltpu.CompilerParams(
            dimension_semantics=("arbitrary", "arbitrary"), vmem_limit_bytes=VMEM_LIMIT),
        name="gdn",
    )(u, u, u, u, small, cw8, cw8, cw8, norm_w)


def _merge_kernel(yg_ref, yd_ref, wa_ref, wb_ref, ga_ref, gb_ref, o_ref):
    pa = _dot(yg_ref[...], wa_ref[...])
    pb = _dot(yd_ref[...], wb_ref[...])
    ga = _sigmoid(ga_ref[...].astype(F32))
    gb = _sigmoid(gb_ref[...].astype(F32))
    o_ref[...] = (ga * pa + gb * pb).astype(o_ref.dtype)


def _merge(y_gla, y_gdn, wb0, wb1, u):
    t, d = y_gla.shape
    tm, tn = 1024, 1024
    return pl.pallas_call(
        _merge_kernel,
        grid=(t // tm, d // tn),
        in_specs=[
            pl.BlockSpec((tm, d), lambda m, n: (m, 0)),
            pl.BlockSpec((tm, d), lambda m, n: (m, 0)),
            pl.BlockSpec((d, tn), lambda m, n: (0, n)),
            pl.BlockSpec((d, tn), lambda m, n: (0, n)),
            pl.BlockSpec((tm, tn), lambda m, n: (m, OFF_MERGE // tn + n)),
            pl.BlockSpec((tm, tn), lambda m, n: (m, (OFF_MERGE + D_MODEL) // tn + n)),
        ],
        out_specs=pl.BlockSpec((tm, tn), lambda m, n: (m, n)),
        out_shape=jax.ShapeDtypeStruct((t, d), BF16),
        compiler_params=pltpu.CompilerParams(
            dimension_semantics=("arbitrary", "arbitrary"), vmem_limit_bytes=VMEM_LIMIT),
        name="merge",
    )(y_gla, y_gdn, wb0, wb1, u, u)


def _outnorm_kernel(m_ref, w_ref, x_ref, gate_ref, lg_ref, lb_ref, o_ref, *, alpha):
    half = m_ref.shape[0] // 2
    outs = [_dot(m_ref[p * half:(p + 1) * half, :], w_ref[...]) for p in range(2)]
    for p, out in enumerate(outs):
        rows = slice(p * half, (p + 1) * half)
        r = alpha * x_ref[rows, :] + gate_ref[0] * out
        mu = jnp.mean(r, axis=-1, keepdims=True)
        rc = r - mu
        var = jnp.mean(rc * rc, axis=-1, keepdims=True)
        o_ref[rows, :] = rc * lax.rsqrt(var + LN_EPS) * lg_ref[...] + lb_ref[...]


def _outnorm(merged, w_out, x2, ada3, ln_g, ln_b, seq, alpha):
    t, d = x2.shape
    tm = 512
    per_b = seq // tm
    return pl.pallas_call(
        functools.partial(_outnorm_kernel, alpha=alpha),
        grid=(t // tm,),
        in_specs=[
            pl.BlockSpec((tm, d), lambda m: (m, 0)),
            pl.BlockSpec((d, d), lambda m: (0, 0)),
            pl.BlockSpec((tm, d), lambda m: (m, 0)),
            pl.BlockSpec((1, 1, d), lambda m: (m // per_b, 0, 2)),
            pl.BlockSpec((1, d), lambda m: (0, 0)),
            pl.BlockSpec((1, d), lambda m: (0, 0)),
        ],
        out_specs=pl.BlockSpec((tm, d), lambda m: (m, 0)),
        out_shape=jax.ShapeDtypeStruct((t, d), F32),
        compiler_params=pltpu.CompilerParams(
            dimension_semantics=("arbitrary",), vmem_limit_bytes=VMEM_LIMIT),
        name="outnorm",
    )(merged, w_out, x2, ada3, ln_g, ln_b)


def _pack_w_in(w_in):
    glr = 2 * GLA_HEADS * GLA_DK + 2 * GLA_HEADS * GLA_DV
    dq = glr + GLA_RANK
    db_ = dq + 4 * GDN_HEADS * GDN_DK
    da_ = db_ + GDN_HEADS
    mg = da_ + GDN_HEADS
    assert glr == OFF_GDN_Q and mg + 2 * D_MODEL == w_in.shape[1]
    pad = jnp.zeros((w_in.shape[0], N_SMALL - GLA_RANK - 2 * GDN_HEADS), w_in.dtype)
    small = jnp.concatenate([w_in[:, glr:dq], w_in[:, db_:da_], w_in[:, da_:mg], pad], axis=1)
    return (w_in[:, :glr].astype(BF16), w_in[:, dq:db_].astype(BF16), w_in[:, mg:].astype(BF16),
            small.astype(BF16))


def _layer(x, c, w_ada, b_ada, w_in, w_gk2, b_gk2, conv_w, a_log, dt_bias,
           gla_norm_w, gdn_norm_w, w_branch, w_out, ln_g, ln_b, alpha):
    bsz, seq, d = x.shape
    t = bsz * seq
    x2 = x.reshape(t, d)

    ada = _ada(c, w_ada, b_ada)
    ada3 = ada.reshape(bsz, 1, 3 * d)

    w_a, w_b, w_c, w_small = _pack_w_in(w_in)
    lane_pad = lambda v: jnp.zeros((1, N_SMALL), F32).at[0, LANE_A:LANE_A + GDN_HEADS].set(v)
    u, small = _inproj(x2, ada3, w_a, w_b, w_c, w_small, lane_pad(a_log), lane_pad(dt_bias), seq)

    w2p = jnp.concatenate(
        [w_gk2, jnp.zeros((N_SMALL - GLA_RANK, w_gk2.shape[1]), w_gk2.dtype)], axis=0).astype(BF16)
    y_gla = _gla(u, small, w2p, b_gk2.reshape(1, -1), gla_norm_w.reshape(1, -1), bsz, seq)

    cw8 = jnp.concatenate([conv_w, jnp.zeros((8 - CONV_K, conv_w.shape[1]), conv_w.dtype)], axis=0)
    y_gdn = _gdn(u, small, cw8, gdn_norm_w.reshape(1, -1), bsz, seq)

    merged = _merge(y_gla, y_gdn, w_branch[0].astype(BF16), w_branch[1].astype(BF16), u)
    out = _outnorm(merged, w_out.astype(BF16), x2, ada3, ln_g.reshape(1, -1), ln_b.reshape(1, -1),
                   seq, alpha)
    return out.reshape(bsz, seq, d)


def kernel(x, c, w_ada, b_ada, w_in, w_gk2, b_gk2, conv_w, a_log, dt_bias, gla_norm_w, gdn_norm_w,
           w_branch, w_out, ln_g, ln_b):
    depth = w_ada.shape[0]
    alpha = (2 * depth) ** 0.25
    for l in range(depth):
        x = _layer(x, c, w_ada[l], b_ada[l], w_in[l], w_gk2[l], b_gk2[l], conv_w[l], a_log[l],
                   dt_bias[l], gla_norm_w[l], gdn_norm_w[l], w_branch[l], w_out[l], ln_g[l], ln_b[l],
                   alpha)
    return x
```

```python
import functools

import jax
import jax.numpy as jnp
from jax import lax
from jax.experimental import pallas as pl
from jax.experimental.pallas import tpu as pltpu

F32 = jnp.float32
BF16 = jnp.bfloat16

D_MODEL = 2048
GLA_HEADS = 4
GLA_DK = 256
GLA_DV = 512
GLA_RANK = 16
GLA_GATE_NORM = 16.0
GDN_HEADS = 16
GDN_DK = 128
GDN_DV = 128
CONV_K = 4
NORM_EPS = 1e-6
LN_EPS = 1e-5

GLA_CHUNK = 64
GLA_GROUP = 8
GLA_PREP_BLOCKS = 4
GDN_CHUNK = 128
GDN_PREP_GROUP = 16
GDN_HEADS_PER_STEP = 2
INV_BASE = 8

OFF_GLA_Q = 0
OFF_GLA_K = 1024
OFF_GLA_V = 2048
OFF_GLA_G = 4096
OFF_GDN_Q = 6144
OFF_GDN_K = 8192
OFF_GDN_V = 10240
OFF_GDN_Z = 12288
OFF_MERGE = 14336
N_MAIN = 18432
N_SMALL = 128
LANE_B = 16
LANE_A = 32

ROW_BLOCK = 256
VMEM_LIMIT = 56 * 1024 * 1024


def _sigmoid(x):
    return 0.5 + 0.5 * jnp.tanh(0.5 * x)


def _silu(x):
    hx = 0.5 * x
    return hx + hx * jnp.tanh(hx)


def _softplus(x):
    return jnp.maximum(x, 0.0) + jnp.log1p(jnp.exp(-jnp.abs(x)))


def _dot(a, b):
    return jnp.dot(a, b, preferred_element_type=F32)


def _dot_nt(a, b):
    return lax.dot_general(a, b, (((1,), (1,)), ((), ())), preferred_element_type=F32)


def _dot_tn(a, b):
    return lax.dot_general(a, b, (((0,), (0,)), ((), ())), preferred_element_type=F32)


def _same_chunk(row, col, chunk):
    shift = chunk.bit_length() - 1
    return jnp.right_shift(row, shift) == jnp.right_shift(col, shift)


def _cumsum_matrix(n, chunk):
    row = lax.broadcasted_iota(jnp.int32, (n, n), 0)
    col = lax.broadcasted_iota(jnp.int32, (n, n), 1)
    return jnp.where(_same_chunk(row, col, chunk) & (col <= row), 1.0, 0.0).astype(BF16)


def _split3(x):
    hi = x.astype(BF16)
    r1 = x - hi.astype(F32)
    lo = r1.astype(BF16)
    lo2 = (r1 - lo.astype(F32)).astype(BF16)
    return hi, lo, lo2


def _dot_exact_lhs(m_bf16, x):
    hi, lo, lo2 = _split3(x)
    return _dot(m_bf16, hi) + _dot(m_bf16, lo) + _dot(m_bf16, lo2)


def _dot_exact_rhs(x, m_bf16):
    hi, lo, lo2 = _split3(x)
    return _dot(hi, m_bf16) + _dot(lo, m_bf16) + _dot(lo2, m_bf16)


def _gated_rmsnorm(o, norm_w, gate):
    ms = jnp.mean(o * o, axis=-1, keepdims=True)
    return o * lax.rsqrt(ms + NORM_EPS) * norm_w * _silu(gate)


def _ada_kernel(c_ref, w_ref, b_ref, o_ref):
    s = _silu(c_ref[...]).astype(BF16)
    o_ref[...] = _dot(s, w_ref[...].astype(BF16)) + b_ref[...]


def _ada(c, w_ada, b_ada):
    bsz, d = c.shape
    n = w_ada.shape[1]
    tn = 768
    return pl.pallas_call(
        _ada_kernel,
        grid=(n // tn,),
        in_specs=[
            pl.BlockSpec((bsz, d), lambda j: (0, 0)),
            pl.BlockSpec((d, tn), lambda j: (0, j)),
            pl.BlockSpec((1, tn), lambda j: (0, j)),
        ],
        out_specs=pl.BlockSpec((bsz, tn), lambda j: (0, j)),
        out_shape=jax.ShapeDtypeStruct((bsz, n), F32),
        compiler_params=pltpu.CompilerParams(
            dimension_semantics=("arbitrary",), vmem_limit_bytes=VMEM_LIMIT),
        name="ada",
    )(c, w_ada, b_ada.reshape(1, n))


def _inproj_kernel(x_ref, sc_ref, sh_ref, wt_ref, wst_ref, al_ref, dt_ref, u_ref, sm_ref, h_scr, *, tm):
    @pl.when(pl.program_id(1) == 0)
    def _():
        sc = 1.0 + sc_ref[0]
        sh = sh_ref[0]
        rb = ROW_BLOCK
        for r in range(tm // rb):
            h = x_ref[r * rb:(r + 1) * rb, :] * sc + sh
            h_scr[r * rb:(r + 1) * rb, :] = h.astype(BF16)
        l_incl = _cumsum_matrix(rb, GDN_CHUNK)
        neg_a = -jnp.exp(al_ref[...])
        lane = lax.broadcasted_iota(jnp.int32, (rb, N_SMALL), 1)
        raw = _dot_nt(h_scr[...], wst_ref[...])
        g = neg_a * _softplus(raw + dt_ref[...])
        parts = [_split3(g[r * rb:(r + 1) * rb, :]) for r in range(tm // rb)]
        cums = [_dot(l_incl, hi) + _dot(l_incl, lo) + _dot(l_incl, lo2) for hi, lo, lo2 in parts]
        for r, cum in enumerate(cums):
            raw_r = raw[r * rb:(r + 1) * rb, :]
            sm_ref[r * rb:(r + 1) * rb, :] = jnp.where(
                lane < LANE_B, raw_r, jnp.where(lane < LANE_A, _sigmoid(raw_r), cum))

    u_ref[...] = _dot_nt(h_scr[...], wt_ref[...]).astype(u_ref.dtype)


def _inproj(x2, ada3, w_t, w_small_t, alog_row, dt_row, seq):
    t, d = x2.shape
    tm, tn = 1024, 1024
    per_b = seq // tm
    nb_a = OFF_GDN_Q // tn
    nb_b = (OFF_MERGE - OFF_GDN_Q) // tn

    def w_row(n):
        skip = jnp.where(n < nb_a, 0, jnp.where(n < nb_a + nb_b, GLA_RANK, GLA_RANK + 2 * GDN_HEADS))
        return pl.multiple_of(n * tn + skip, 16)

    return pl.pallas_call(
        functools.partial(_inproj_kernel, tm=tm),
        grid=(t // tm, N_MAIN // tn),
        in_specs=[
            pl.BlockSpec((tm, d), lambda m, n: (m, 0)),
            pl.BlockSpec((1, 1, d), lambda m, n: (m // per_b, 0, 1)),
            pl.BlockSpec((1, 1, d), lambda m, n: (m // per_b, 0, 0)),
            pl.BlockSpec((pl.Element(tn), pl.Element(d)), lambda m, n: (w_row(n), 0)),
            pl.BlockSpec((N_SMALL, d), lambda m, n: (0, 0)),
            pl.BlockSpec((1, N_SMALL), lambda m, n: (0, 0)),
            pl.BlockSpec((1, N_SMALL), lambda m, n: (0, 0)),
        ],
        out_specs=[
            pl.BlockSpec((tm, tn), lambda m, n: (m, n)),
            pl.BlockSpec((tm, N_SMALL), lambda m, n: (m, 0)),
        ],
        out_shape=[
            jax.ShapeDtypeStruct((t, N_MAIN), BF16),
            jax.ShapeDtypeStruct((t, N_SMALL), F32),
        ],
        scratch_shapes=[pltpu.VMEM((tm, d), BF16)],
        compiler_params=pltpu.CompilerParams(
            dimension_semantics=("arbitrary", "arbitrary"), vmem_limit_bytes=VMEM_LIMIT),
        name="inproj",
    )(x2, ada3, ada3, w_t, w_small_t, alog_row, dt_row)


def _gla_kernel(q_ref, k_ref, v_ref, g_ref, sm_ref, w2_ref, b2_ref, nw_ref, y_ref,
                qe_s, ki_s, kd_s, dl_s, st_s, *, seq):
    c = GLA_CHUNK
    rb = ROW_BLOCK
    cpb = rb // c
    l_incl = _cumsum_matrix(rb, c)
    scale = GLA_DK ** -0.5

    pg = GLA_PREP_BLOCKS

    def prep(i, carry):
        blks = [i * pg + b for b in range(pg)]
        rws = [pl.ds(pl.multiple_of(blk * rb, rb), rb) for blk in blks]
        zs = [_dot(sm_ref[r, :].astype(BF16), w2_ref[...]) + b2_ref[...] for r in rws]
        las = [-_softplus(-z) * (1.0 / GLA_GATE_NORM) for z in zs]
        his = [la.astype(BF16) for la in las]
        los = [(la - hi.astype(F32)).astype(BF16) for la, hi in zip(las, his)]
        cums = [_dot(l_incl, hi) + _dot(l_incl, lo) for hi, lo in zip(his, los)]
        for blk, r, cum in zip(blks, rws, cums):
            lasts = [cum[j * c + c - 1:j * c + c, :] for j in range(cpb)]
            elast = jnp.concatenate([jnp.broadcast_to(jnp.exp(l), (c, GLA_DK)) for l in lasts], axis=0)
            q = q_ref[r, :].astype(F32)
            k = k_ref[r, :].astype(F32)
            ki = k * jnp.exp(-cum)
            qe_s[r, :] = (q * (jnp.exp(cum) * scale)).astype(BF16)
            ki_s[r, :] = ki.astype(BF16)
            kd_s[r, :] = (ki * elast).astype(BF16)
            for j in range(cpb):
                dl_s[blk * cpb + j] = elast[j * c:j * c + 8, :]
        return carry

    lax.fori_loop(0, seq // (rb * pg), prep, 0)

    st_s[...] = jnp.zeros_like(st_s)
    tr = lax.broadcasted_iota(jnp.int32, (c, c), 0)
    tc = lax.broadcasted_iota(jnp.int32, (c, c), 1)
    causal = tc <= tr
    grp = GLA_GROUP

    def step(i, carry):
        ns = [i * grp + j for j in range(grp)]
        rws = [pl.ds(pl.multiple_of(n * c, c), c) for n in ns]
        qes = [qe_s[r, :] for r in rws]
        vs = [v_ref[r, :] for r in rws]
        attns = [jnp.where(causal, _dot_nt(qe, ki_s[r, :]), 0.0).astype(BF16) for qe, r in zip(qes, rws)]
        upds = [_dot_tn(v, kd_s[r, :]) for v, r in zip(vs, rws)]
        st = st_s[...]
        sts = []
        for j, n in enumerate(ns):
            sts.append(st.astype(BF16))
            st = st * dl_s[n][0:1, :] + upds[j]
        st_s[...] = st
        for j, r in enumerate(rws):
            o = _dot_nt(qes[j], sts[j]) + _dot(attns[j], vs[j])
            y = _gated_rmsnorm(o, nw_ref[...], g_ref[r, :].astype(F32))
            y_ref[r, :] = y.astype(y_ref.dtype)
        return carry

    lax.fori_loop(0, seq // (c * grp), step, 0)


def _gla(u, small, w2p, b2, norm_w, bsz, seq):
    t = u.shape[0]
    qb, vb = GLA_DK, GLA_DV
    return pl.pallas_call(
        functools.partial(_gla_kernel, seq=seq),
        grid=(bsz, GLA_HEADS),
        in_specs=[
            pl.BlockSpec((seq, qb), lambda b, h: (b, OFF_GLA_Q // qb + h)),
            pl.BlockSpec((seq, qb), lambda b, h: (b, OFF_GLA_K // qb + h)),
            pl.BlockSpec((seq, vb), lambda b, h: (b, OFF_GLA_V // vb + h)),
            pl.BlockSpec((seq, vb), lambda b, h: (b, OFF_GLA_G // vb + h)),
            pl.BlockSpec((seq, N_SMALL), lambda b, h: (b, 0)),
            pl.BlockSpec((N_SMALL, qb), lambda b, h: (0, h)),
            pl.BlockSpec((1, qb), lambda b, h: (0, h)),
            pl.BlockSpec((1, vb), lambda b, h: (0, 0)),
        ],
        out_specs=pl.BlockSpec((seq, vb), lambda b, h: (b, h)),
        out_shape=jax.ShapeDtypeStruct((t, GLA_HEADS * GLA_DV), BF16),
        scratch_shapes=[
            pltpu.VMEM((seq, qb), BF16),
            pltpu.VMEM((seq, qb), BF16),
            pltpu.VMEM((seq, qb), BF16),
            pltpu.VMEM((seq // GLA_CHUNK, 8, qb), F32),
            pltpu.VMEM((vb, qb), F32),
        ],
        compiler_params=pltpu.CompilerParams(
            dimension_semantics=("arbitrary", "arbitrary"), vmem_limit_bytes=VMEM_LIMIT),
        name="gla",
    )(u, u, u, u, small, w2p, b2, norm_w)


def _unit_lower_inverse(ms, eye, tr, tc):
    n = ms[0].shape[0]
    base = _same_chunk(tr, tc, INV_BASE)
    ps = [jnp.where(base, -m, 0.0) for m in ms]
    tinvs = [eye + p for p in ps]
    size = 2
    while size < INV_BASE:
        pbs = [p.astype(BF16) for p in ps]
        ps = [_dot(pb, pb) for pb in pbs]
        tinvs = [t + _dot(t.astype(BF16), p.astype(BF16)) for t, p in zip(tinvs, ps)]
        size *= 2
    blk = INV_BASE
    while blk < n:
        off = _same_chunk(tr, tc, 2 * blk) & jnp.logical_not(_same_chunk(tr, tc, blk))
        moffs = [jnp.where(off, m, 0.0).astype(BF16) for m in ms]
        tbs = [t.astype(BF16) for t in tinvs]
        mids = [_dot(tb, mo).astype(BF16) for tb, mo in zip(tbs, moffs)]
        tinvs = [t - _dot(mid, tb) for t, mid, tb in zip(tinvs, mids, tbs)]
        blk *= 2
    return tinvs


def _gdn_kernel(q_ref, k_ref, v_ref, z_ref, sm_ref, cwq_ref, cwk_ref, cwv_ref, nw_ref, y_ref,
                xq_s, xk_s, xv_s, q_s, k_s, v_s, bc_s, cc_s,
                u_s, wq_s, a_s, ws_s, bs_s, gl_s, *, seq):
    c = GDN_CHUNK
    nchunk = seq // c
    rb = ROW_BLOCK
    dk = GDN_DK
    scale = GDN_DK ** -0.5

    er = lax.broadcasted_iota(jnp.int32, (N_SMALL, 2 * dk), 0)
    ec = lax.broadcasted_iota(jnp.int32, (N_SMALL, 2 * dk), 1)
    sel_lane = jnp.where(ec < dk, LANE_B, LANE_A)
    tr = lax.broadcasted_iota(jnp.int32, (c, c), 0)
    tc = lax.broadcasted_iota(jnp.int32, (c, c), 1)
    lower = tc <= tr
    strict = tc < tr
    eye = jnp.where(tc == tr, 1.0, 0.0).astype(F32)

    def conv_silu(xs, cw_ref, lanes, r0):
        acc = None
        for j in range(CONV_K):
            lo = r0 + 8 - (CONV_K - 1) + j
            tap = xs[lo:lo + rb, :] * cw_ref[j:j + 1, lanes]
            acc = tap if acc is None else acc + tap
        return _silu(acc)

    def l2n(x):
        return x * lax.rsqrt(jnp.sum(x * x, axis=-1, keepdims=True) + NORM_EPS)

    for hp in range(GDN_HEADS_PER_STEP):
        lanes = slice(hp * dk, (hp + 1) * dk)
        head = pl.program_id(1) * GDN_HEADS_PER_STEP + hp

        for src, dst in ((q_ref, xq_s), (k_ref, xk_s), (v_ref, xv_s)):
            dst[0:8, :] = jnp.zeros((8, dk), F32)
            dst[8:8 + seq, :] = src[:, lanes].astype(F32)

        sel = jnp.where(er == sel_lane + head, 1.0, 0.0).astype(BF16)

        for i in range(seq // rb):
            r0 = i * rb
            rows = slice(r0, r0 + rb)
            q_s[rows, :] = l2n(conv_silu(xq_s, cwq_ref, lanes, r0)) * scale
            k_s[rows, :] = l2n(conv_silu(xk_s, cwk_ref, lanes, r0))
            v_s[rows, :] = conv_silu(xv_s, cwv_ref, lanes, r0)
            both = _dot_exact_rhs(sm_ref[rows, :], sel)
            bc_s[rows, :] = both[:, :dk]
            cc_s[rows, :] = both[:, dk:]

        for g0 in range(0, nchunk, GDN_PREP_GROUP):
            ns = list(range(g0, g0 + GDN_PREP_GROUP))
            rws = [slice(n * c, (n + 1) * c) for n in ns]
            qs = [q_s[r, :] for r in rws]
            ks = [k_s[r, :] for r in rws]
            vs = [v_s[r, :].astype(BF16) for r in rws]
            bcols = [bc_s[r, :] for r in rws]
            ccols = [cc_s[r, :] for r in rws]
            brows = [b.T for b in bcols]
            crows = [cc.T for cc in ccols]
            gams = [jnp.where(lower, jnp.exp(jnp.where(lower, cc - cr, 0.0)), 0.0)
                    for cc, cr in zip(ccols, crows)]
            kbs = [k.astype(BF16) for k in ks]
            kks = [_dot_nt(kb, kb) for kb in kbs]
            qks = [_dot_nt(q.astype(BF16), kb) for q, kb in zip(qs, kbs)]
            ms = [jnp.where(strict, b * kk * g, 0.0) for b, kk, g in zip(bcols, kks, gams)]
            tinvs = _unit_lower_inverse(ms, eye, tr, tc)
            tbs = [t * b for t, b in zip(tinvs, brows)]
            us = [_dot(tb.astype(BF16), v) for tb, v in zip(tbs, vs)]
            ws = [_dot((tb * jnp.exp(cr)).astype(BF16), kb).astype(BF16)
                  for tb, cr, kb in zip(tbs, crows, kbs)]
            lastrows = [cc[c - 1:c, :] for cc in ccols]
            kdts = [(k * jnp.exp(l - cc)).T.astype(BF16) for k, l, cc in zip(ks, lastrows, ccols)]
            wss = [_dot(kdt, w) for kdt, w in zip(kdts, ws)]
            bss = [_dot(kdt, u.astype(BF16)) for kdt, u in zip(kdts, us)]
            for j, n in enumerate(ns):
                u_s[hp, n] = us[j]
                wq_s[hp, n, 0:c, :] = ws[j]
                wq_s[hp, n, c:2 * c, :] = (qs[j] * jnp.exp(ccols[j])).astype(BF16)
                a_s[hp, n] = (qks[j] * gams[j]).astype(BF16)
                ws_s[hp, n] = wss[j].astype(BF16)
                bs_s[hp, n] = bss[j]
                gl_s[hp, n] = jnp.broadcast_to(jnp.exp(lastrows[j]), (8, GDN_DV))

    sts = [jnp.zeros((dk, GDN_DV), F32) for _ in range(GDN_HEADS_PER_STEP)]
    for n in range(nchunk):
        rows = slice(n * c, (n + 1) * c)
        for hp in range(GDN_HEADS_PER_STEP):
            lanes = slice(hp * dk, (hp + 1) * dk)
            sb = sts[hp].astype(BF16)
            sts[hp] = sts[hp] * gl_s[hp, n][0:1, :] - _dot(ws_s[hp, n], sb) + bs_s[hp, n]
            wq = _dot(wq_s[hp, n], sb)
            vb = (u_s[hp, n] - wq[0:c, :]).astype(BF16)
            o = wq[c:2 * c, :] + _dot(a_s[hp, n], vb)
            y = _gated_rmsnorm(o, nw_ref[...], z_ref[rows, lanes].astype(F32))
            y_ref[rows, lanes] = y.astype(y_ref.dtype)


def _gdn(u, small, cw8, norm_w, bsz, seq):
    t = u.shape[0]
    dk = GDN_DK
    hps = GDN_HEADS_PER_STEP
    wb = hps * dk
    nchunk = seq // GDN_CHUNK
    c = GDN_CHUNK
    conv_blocks = GDN_HEADS // hps
    return pl.pallas_call(
        functools.partial(_gdn_kernel, seq=seq),
        grid=(bsz, GDN_HEADS // hps),
        in_specs=[
            pl.BlockSpec((seq, wb), lambda b, p: (b, OFF_GDN_Q // wb + p)),
            pl.BlockSpec((seq, wb), lambda b, p: (b, OFF_GDN_K // wb + p)),
            pl.BlockSpec((seq, wb), lambda b, p: (b, OFF_GDN_V // wb + p)),
            pl.BlockSpec((seq, wb), lambda b, p: (b, OFF_GDN_Z // wb + p)),
            pl.BlockSpec((seq, N_SMALL), lambda b, p: (b, 0)),
            pl.BlockSpec((8, wb), lambda b, p: (0, p)),
            pl.BlockSpec((8, wb), lambda b, p: (0, conv_blocks + p)),
            pl.BlockSpec((8, wb), lambda b, p: (0, 2 * conv_blocks + p)),
            pl.BlockSpec((1, dk), lambda b, p: (0, 0)),
        ],
        out_specs=pl.BlockSpec((seq, wb), lambda b, p: (b, p)),
        out_shape=jax.ShapeDtypeStruct((t, GDN_HEADS * GDN_DV), BF16),
        scratch_shapes=[
            pltpu.VMEM((seq + 8, dk), F32),
            pltpu.VMEM((seq + 8, dk), F32),
            pltpu.VMEM((seq + 8, dk), F32),
            pltpu.VMEM((seq, dk), F32),
            pltpu.VMEM((seq, dk), F32),
            pltpu.VMEM((seq, dk), F32),
            pltpu.VMEM((seq, dk), F32),
            pltpu.VMEM((seq, dk), F32),
            pltpu.VMEM((hps, nchunk, c, dk), F32),
            pltpu.VMEM((hps, nchunk, 2 * c, dk), BF16),
            pltpu.VMEM((hps, nchunk, c, c), BF16),
            pltpu.VMEM((hps, nchunk, dk, dk), BF16),
            pltpu.VMEM((hps, nchunk, dk, dk), F32),
            pltpu.VMEM((hps, nchunk, 8, dk), F32),
        ],
        compiler_params=pltpu.CompilerParams(
            dimension_semantics=("arbitrary", "arbitrary"), vmem_limit_bytes=VMEM_LIMIT),
        name="gdn",
    )(u, u, u, u, small, cw8, cw8, cw8, norm_w)


def _merge_kernel(yg_ref, yd_ref, wa_ref, wb_ref, ga_ref, gb_ref, o_ref):
    pa = _dot(yg_ref[...], wa_ref[...])
    pb = _dot(yd_ref[...], wb_ref[...])
    ga = _sigmoid(ga_ref[...].astype(F32))
    gb = _sigmoid(gb_ref[...].astype(F32))
    o_ref[...] = (ga * pa + gb * pb).astype(o_ref.dtype)


def _merge(y_gla, y_gdn, w_branch, u):
    t, d = y_gla.shape
    tm, tn = 1024, 1024
    return pl.pallas_call(
        _merge_kernel,
        grid=(t // tm, d // tn),
        in_specs=[
            pl.BlockSpec((tm, d), lambda m, n: (m, 0)),
            pl.BlockSpec((tm, d), lambda m, n: (m, 0)),
            pl.BlockSpec((None, d, tn), lambda m, n: (0, 0, n)),
            pl.BlockSpec((None, d, tn), lambda m, n: (1, 0, n)),
            pl.BlockSpec((tm, tn), lambda m, n: (m, OFF_MERGE // tn + n)),
            pl.BlockSpec((tm, tn), lambda m, n: (m, (OFF_MERGE + D_MODEL) // tn + n)),
        ],
        out_specs=pl.BlockSpec((tm, tn), lambda m, n: (m, n)),
        out_shape=jax.ShapeDtypeStruct((t, d), BF16),
        compiler_params=pltpu.CompilerParams(
            dimension_semantics=("arbitrary", "arbitrary"), vmem_limit_bytes=VMEM_LIMIT),
        name="merge",
    )(y_gla, y_gdn, w_branch, w_branch, u, u)


def _outnorm_kernel(m_ref, w_ref, x_ref, gate_ref, lg_ref, lb_ref, o_ref, *, alpha):
    half = m_ref.shape[0] // 2
    outs = [_dot(m_ref[p * half:(p + 1) * half, :], w_ref[...]) for p in range(2)]
    for p, out in enumerate(outs):
        rows = slice(p * half, (p + 1) * half)
        r = alpha * x_ref[rows, :] + gate_ref[0] * out
        mu = jnp.mean(r, axis=-1, keepdims=True)
        rc = r - mu
        var = jnp.mean(rc * rc, axis=-1, keepdims=True)
        o_ref[rows, :] = rc * lax.rsqrt(var + LN_EPS) * lg_ref[...] + lb_ref[...]


def _outnorm(merged, w_out, x2, ada3, ln_g, ln_b, seq, alpha):
    t, d = x2.shape
    tm = 512
    per_b = seq // tm
    return pl.pallas_call(
        functools.partial(_outnorm_kernel, alpha=alpha),
        grid=(t // tm,),
        in_specs=[
            pl.BlockSpec((tm, d), lambda m: (m, 0)),
            pl.BlockSpec((d, d), lambda m: (0, 0)),
            pl.BlockSpec((tm, d), lambda m: (m, 0)),
            pl.BlockSpec((1, 1, d), lambda m: (m // per_b, 0, 2)),
            pl.BlockSpec((1, d), lambda m: (0, 0)),
            pl.BlockSpec((1, d), lambda m: (0, 0)),
        ],
        out_specs=pl.BlockSpec((tm, d), lambda m: (m, 0)),
        out_shape=jax.ShapeDtypeStruct((t, d), F32),
        compiler_params=pltpu.CompilerParams(
            dimension_semantics=("arbitrary",), vmem_limit_bytes=VMEM_LIMIT),
        name="outnorm",
    )(merged, w_out, x2, ada3, ln_g, ln_b)


def _pack_w_in(w_in):
    w_t = jnp.transpose(w_in).astype(BF16)
    glr = OFF_GDN_Q
    db_ = glr + GLA_RANK + 4 * GDN_HEADS * GDN_DK
    mg = db_ + 2 * GDN_HEADS
    assert mg + 2 * D_MODEL == w_in.shape[1]
    pad = jnp.zeros((N_SMALL - GLA_RANK - 2 * GDN_HEADS, w_in.shape[0]), BF16)
    small_t = jnp.concatenate([w_t[glr:glr + GLA_RANK], w_t[db_:mg], pad], axis=0)
    return w_t, small_t


def _layer(x, c, w_ada, b_ada, w_in, w_gk2, b_gk2, conv_w, a_log, dt_bias,
           gla_norm_w, gdn_norm_w, w_branch, w_out, ln_g, ln_b, alpha):
    bsz, seq, d = x.shape
    t = bsz * seq
    x2 = x.reshape(t, d)

    ada = _ada(c, w_ada, b_ada)
    ada3 = ada.reshape(bsz, 1, 3 * d)

    w_t, w_small_t = _pack_w_in(w_in)
    lane_pad = lambda v: jnp.zeros((1, N_SMALL), F32).at[0, LANE_A:LANE_A + GDN_HEADS].set(v)
    u, small = _inproj(x2, ada3, w_t, w_small_t, lane_pad(a_log), lane_pad(dt_bias), seq)

    w2p = jnp.concatenate(
        [w_gk2, jnp.zeros((N_SMALL - GLA_RANK, w_gk2.shape[1]), w_gk2.dtype)], axis=0).astype(BF16)
    y_gla = _gla(u, small, w2p, b_gk2.reshape(1, -1), gla_norm_w.reshape(1, -1), bsz, seq)

    cw8 = jnp.concatenate([conv_w, jnp.zeros((8 - CONV_K, conv_w.shape[1]), conv_w.dtype)], axis=0)
    y_gdn = _gdn(u, small, cw8, gdn_norm_w.reshape(1, -1), bsz, seq)

    merged = _merge(y_gla, y_gdn, w_branch.astype(BF16), u)
    out = _outnorm(merged, w_out.astype(BF16), x2, ada3, ln_g.reshape(1, -1), ln_b.reshape(1, -1),
                   seq, alpha)
    return out.reshape(bsz, seq, d)


def kernel(x, c, w_ada, b_ada, w_in, w_gk2, b_gk2, conv_w, a_log, dt_bias, gla_norm_w, gdn_norm_w,
           w_branch, w_out, ln_g, ln_b):
    depth = w_ada.shape[0]
    alpha = (2 * depth) ** 0.25
    for l in range(depth):
        x = _layer(x, c, w_ada[l], b_ada[l], w_in[l], w_gk2[l], b_gk2[l], conv_w[l], a_log[l],
                   dt_bias[l], gla_norm_w[l], gdn_norm_w[l], w_branch[l], w_out[l], ln_g[l], ln_b[l],
                   alpha)
    return x
```

```python
import functools

import jax
import jax.numpy as jnp
from jax import lax
from jax.experimental import pallas as pl
from jax.experimental.pallas import tpu as pltpu

F32 = jnp.float32
BF16 = jnp.bfloat16

D_MODEL = 2048
GLA_HEADS = 4
GLA_DK = 256
GLA_DV = 512
GLA_RANK = 16
GLA_GATE_NORM = 16.0
GDN_HEADS = 16
GDN_DK = 128
GDN_DV = 128
CONV_K = 4
NORM_EPS = 1e-6
LN_EPS = 1e-5

GLA_CHUNK = 64
GLA_GROUP = 8
GLA_PREP_BLOCKS = 4
GDN_CHUNK = 128
GDN_PREP_GROUP = 16
GDN_HEADS_PER_STEP = 2
INV_BASE = 8

OFF_GLA_Q = 0
OFF_GLA_K = 1024
OFF_GLA_V = 2048
OFF_GLA_G = 4096
OFF_GDN_Q = 6144
OFF_GDN_K = 8192
OFF_GDN_V = 10240
OFF_GDN_Z = 12288
OFF_MERGE = 14336
N_MAIN = 18432
N_SMALL = 128
LANE_B = 16
LANE_A = 32

ROW_BLOCK = 256
VMEM_LIMIT = 56 * 1024 * 1024


def _sigmoid(x):
    return 0.5 + 0.5 * jnp.tanh(0.5 * x)


def _silu(x):
    hx = 0.5 * x
    return hx + hx * jnp.tanh(hx)


def _softplus(x):
    return jnp.maximum(x, 0.0) + jnp.log1p(jnp.exp(-jnp.abs(x)))


def _dot(a, b):
    return jnp.dot(a, b, preferred_element_type=F32)


def _dot_nt(a, b):
    return lax.dot_general(a, b, (((1,), (1,)), ((), ())), preferred_element_type=F32)


def _dot_tn(a, b):
    return lax.dot_general(a, b, (((0,), (0,)), ((), ())), preferred_element_type=F32)


def _same_chunk(row, col, chunk):
    shift = chunk.bit_length() - 1
    return jnp.right_shift(row, shift) == jnp.right_shift(col, shift)


def _cumsum_matrix(n, chunk):
    row = lax.broadcasted_iota(jnp.int32, (n, n), 0)
    col = lax.broadcasted_iota(jnp.int32, (n, n), 1)
    return jnp.where(_same_chunk(row, col, chunk) & (col <= row), 1.0, 0.0).astype(BF16)


def _split3(x):
    hi = x.astype(BF16)
    r1 = x - hi.astype(F32)
    lo = r1.astype(BF16)
    lo2 = (r1 - lo.astype(F32)).astype(BF16)
    return hi, lo, lo2


def _dot_exact_lhs(m_bf16, x):
    hi, lo, lo2 = _split3(x)
    return _dot(m_bf16, hi) + _dot(m_bf16, lo) + _dot(m_bf16, lo2)


def _dot_exact_rhs(x, m_bf16):
    hi, lo, lo2 = _split3(x)
    return _dot(hi, m_bf16) + _dot(lo, m_bf16) + _dot(lo2, m_bf16)


def _gated_rmsnorm(o, norm_w, gate):
    ms = jnp.mean(o * o, axis=-1, keepdims=True)
    return o * lax.rsqrt(ms + NORM_EPS) * norm_w * _silu(gate)


def _ada_kernel(c_ref, w_ref, b_ref, o_ref):
    s = _silu(c_ref[...]).astype(BF16)
    o_ref[...] = _dot(s, w_ref[...].astype(BF16)) + b_ref[...]


def _ada(c, w_ada, b_ada):
    bsz, d = c.shape
    n = w_ada.shape[1]
    tn = 768
    return pl.pallas_call(
        _ada_kernel,
        grid=(n // tn,),
        in_specs=[
            pl.BlockSpec((bsz, d), lambda j: (0, 0)),
            pl.BlockSpec((d, tn), lambda j: (0, j)),
            pl.BlockSpec((1, tn), lambda j: (0, j)),
        ],
        out_specs=pl.BlockSpec((bsz, tn), lambda j: (0, j)),
        out_shape=jax.ShapeDtypeStruct((bsz, n), F32),
        compiler_params=pltpu.CompilerParams(
            dimension_semantics=("arbitrary",), vmem_limit_bytes=VMEM_LIMIT),
        name="ada",
    )(c, w_ada, b_ada.reshape(1, n))


def _inproj_kernel(x_ref, sc_ref, sh_ref, wt_ref, wst_ref, al_ref, dt_ref, u_ref, sm_ref, h_scr, *, tm):
    @pl.when(pl.program_id(1) == 0)
    def _():
        sc = 1.0 + sc_ref[0]
        sh = sh_ref[0]
        rb = ROW_BLOCK
        for r in range(tm // rb):
            h = x_ref[r * rb:(r + 1) * rb, :] * sc + sh
            h_scr[r * rb:(r + 1) * rb, :] = h.astype(BF16)
        l_incl = _cumsum_matrix(rb, GDN_CHUNK)
        neg_a = -jnp.exp(al_ref[...])
        lane = lax.broadcasted_iota(jnp.int32, (rb, N_SMALL), 1)
        raw = _dot_nt(h_scr[...], wst_ref[...])
        g = neg_a * _softplus(raw + dt_ref[...])
        parts = [_split3(g[r * rb:(r + 1) * rb, :]) for r in range(tm // rb)]
        cums = [_dot(l_incl, hi) + _dot(l_incl, lo) + _dot(l_incl, lo2) for hi, lo, lo2 in parts]
        for r, cum in enumerate(cums):
            raw_r = raw[r * rb:(r + 1) * rb, :]
            sm_ref[r * rb:(r + 1) * rb, :] = jnp.where(
                lane < LANE_B, raw_r, jnp.where(lane < LANE_A, _sigmoid(raw_r), cum))

    u_ref[...] = _dot_nt(h_scr[...], wt_ref[...]).astype(u_ref.dtype)


def _inproj(x2, ada3, w_t, w_small_t, alog_row, dt_row, seq):
    t, d = x2.shape
    tm, tn = 1024, 2048
    per_b = seq // tm
    nb_a = OFF_GDN_Q // tn
    nb_b = (OFF_MERGE - OFF_GDN_Q) // tn

    def w_row(n):
        skip = jnp.where(n < nb_a, 0, jnp.where(n < nb_a + nb_b, GLA_RANK, GLA_RANK + 2 * GDN_HEADS))
        return pl.multiple_of(n * tn + skip, 16)

    return pl.pallas_call(
        functools.partial(_inproj_kernel, tm=tm),
        grid=(t // tm, N_MAIN // tn),
        in_specs=[
            pl.BlockSpec((tm, d), lambda m, n: (m, 0)),
            pl.BlockSpec((1, 1, d), lambda m, n: (m // per_b, 0, 1)),
            pl.BlockSpec((1, 1, d), lambda m, n: (m // per_b, 0, 0)),
            pl.BlockSpec((pl.Element(tn), pl.Element(d)), lambda m, n: (w_row(n), 0)),
            pl.BlockSpec((N_SMALL, d), lambda m, n: (0, 0)),
            pl.BlockSpec((1, N_SMALL), lambda m, n: (0, 0)),
            pl.BlockSpec((1, N_SMALL), lambda m, n: (0, 0)),
        ],
        out_specs=[
            pl.BlockSpec((tm, tn), lambda m, n: (m, n)),
            pl.BlockSpec((tm, N_SMALL), lambda m, n: (m, 0)),
        ],
        out_shape=[
            jax.ShapeDtypeStruct((t, N_MAIN), BF16),
            jax.ShapeDtypeStruct((t, N_SMALL), F32),
        ],
        scratch_shapes=[pltpu.VMEM((tm, d), BF16)],
        compiler_params=pltpu.CompilerParams(
            dimension_semantics=("arbitrary", "arbitrary"), vmem_limit_bytes=VMEM_LIMIT),
        name="inproj",
    )(x2, ada3, ada3, w_t, w_small_t, alog_row, dt_row)


def _gla_kernel(q_ref, k_ref, v_ref, g_ref, sm_ref, w2_ref, b2_ref, nw_ref, y_ref,
                qe_s, ki_s, kd_s, dl_s, st_s, *, seq):
    c = GLA_CHUNK
    rb = ROW_BLOCK
    cpb = rb // c
    l_incl = _cumsum_matrix(rb, c)
    scale = GLA_DK ** -0.5

    pg = GLA_PREP_BLOCKS

    def prep(i, carry):
        blks = [i * pg + b for b in range(pg)]
        rws = [pl.ds(pl.multiple_of(blk * rb, rb), rb) for blk in blks]
        zs = [_dot(sm_ref[r, :].astype(BF16), w2_ref[...]) + b2_ref[...] for r in rws]
        las = [-_softplus(-z) * (1.0 / GLA_GATE_NORM) for z in zs]
        his = [la.astype(BF16) for la in las]
        los = [(la - hi.astype(F32)).astype(BF16) for la, hi in zip(las, his)]
        cums = [_dot(l_incl, hi) + _dot(l_incl, lo) for hi, lo in zip(his, los)]
        for blk, r, cum in zip(blks, rws, cums):
            lasts = [cum[j * c + c - 1:j * c + c, :] for j in range(cpb)]
            elast = jnp.concatenate([jnp.broadcast_to(jnp.exp(l), (c, GLA_DK)) for l in lasts], axis=0)
            q = q_ref[r, :].astype(F32)
            k = k_ref[r, :].astype(F32)
            ki = k * jnp.exp(-cum)
            qe_s[r, :] = (q * (jnp.exp(cum) * scale)).astype(BF16)
            ki_s[r, :] = ki.astype(BF16)
            kd_s[r, :] = (ki * elast).astype(BF16)
            for j in range(cpb):
                dl_s[blk * cpb + j] = elast[j * c:j * c + 8, :]
        return carry

    lax.fori_loop(0, seq // (rb * pg), prep, 0)

    st_s[...] = jnp.zeros_like(st_s)
    tr = lax.broadcasted_iota(jnp.int32, (c, c), 0)
    tc = lax.broadcasted_iota(jnp.int32, (c, c), 1)
    causal = tc <= tr
    grp = GLA_GROUP

    def step(i, carry):
        ns = [i * grp + j for j in range(grp)]
        rws = [pl.ds(pl.multiple_of(n * c, c), c) for n in ns]
        qes = [qe_s[r, :] for r in rws]
        vs = [v_ref[r, :] for r in rws]
        attns = [jnp.where(causal, _dot_nt(qe, ki_s[r, :]), 0.0).astype(BF16) for qe, r in zip(qes, rws)]
        upds = [_dot_tn(v, kd_s[r, :]) for v, r in zip(vs, rws)]
        st = st_s[...]
        sts = []
        for j, n in enumerate(ns):
            sts.append(st.astype(BF16))
            st = st * dl_s[n][0:1, :] + upds[j]
        st_s[...] = st
        for j, r in enumerate(rws):
            o = _dot_nt(qes[j], sts[j]) + _dot(attns[j], vs[j])
            y = _gated_rmsnorm(o, nw_ref[...], g_ref[r, :].astype(F32))
            y_ref[r, :] = y.astype(y_ref.dtype)
        return carry

    lax.fori_loop(0, seq // (c * grp), step, 0)


def _gla(u, small, w2p, b2, norm_w, bsz, seq):
    t = u.shape[0]
    qb, vb = GLA_DK, GLA_DV
    return pl.pallas_call(
        functools.partial(_gla_kernel, seq=seq),
        grid=(bsz, GLA_HEADS),
        in_specs=[
            pl.BlockSpec((seq, qb), lambda b, h: (b, OFF_GLA_Q // qb + h)),
            pl.BlockSpec((seq, qb), lambda b, h: (b, OFF_GLA_K // qb + h)),
            pl.BlockSpec((seq, vb), lambda b, h: (b, OFF_GLA_V // vb + h)),
            pl.BlockSpec((seq, vb), lambda b, h: (b, OFF_GLA_G // vb + h)),
            pl.BlockSpec((seq, N_SMALL), lambda b, h: (b, 0)),
            pl.BlockSpec((N_SMALL, qb), lambda b, h: (0, h)),
            pl.BlockSpec((1, qb), lambda b, h: (0, h)),
            pl.BlockSpec((1, vb), lambda b, h: (0, 0)),
        ],
        out_specs=pl.BlockSpec((seq, vb), lambda b, h: (b, h)),
        out_shape=jax.ShapeDtypeStruct((t, GLA_HEADS * GLA_DV), BF16),
        scratch_shapes=[
            pltpu.VMEM((seq, qb), BF16),
            pltpu.VMEM((seq, qb), BF16),
            pltpu.VMEM((seq, qb), BF16),
            pltpu.VMEM((seq // GLA_CHUNK, 8, qb), F32),
            pltpu.VMEM((vb, qb), F32),
        ],
        compiler_params=pltpu.CompilerParams(
            dimension_semantics=("arbitrary", "arbitrary"), vmem_limit_bytes=VMEM_LIMIT),
        name="gla",
    )(u, u, u, u, small, w2p, b2, norm_w)


def _unit_lower_inverse(ms, eye, tr, tc):
    n = ms[0].shape[0]
    base = _same_chunk(tr, tc, INV_BASE)
    ps = [jnp.where(base, -m, 0.0) for m in ms]
    tinvs = [eye + p for p in ps]
    size = 2
    while size < INV_BASE:
        pbs = [p.astype(BF16) for p in ps]
        ps = [_dot(pb, pb) for pb in pbs]
        tinvs = [t + _dot(t.astype(BF16), p.astype(BF16)) for t, p in zip(tinvs, ps)]
        size *= 2
    blk = INV_BASE
    while blk < n:
        off = _same_chunk(tr, tc, 2 * blk) & jnp.logical_not(_same_chunk(tr, tc, blk))
        moffs = [jnp.where(off, m, 0.0).astype(BF16) for m in ms]
        tbs = [t.astype(BF16) for t in tinvs]
        mids = [_dot(tb, mo).astype(BF16) for tb, mo in zip(tbs, moffs)]
        tinvs = [t - _dot(mid, tb) for t, mid, tb in zip(tinvs, mids, tbs)]
        blk *= 2
    return tinvs


def _gdn_kernel(q_ref, k_ref, v_ref, z_ref, sm_ref, cwq_ref, cwk_ref, cwv_ref, nw_ref, y_ref,
                xq_s, xk_s, xv_s, q_s, k_s, v_s, bc_s, cc_s,
                u_s, wq_s, a_s, ws_s, bs_s, gl_s, *, seq):
    c = GDN_CHUNK
    nchunk = seq // c
    rb = ROW_BLOCK
    dk = GDN_DK
    scale = GDN_DK ** -0.5

    er = lax.broadcasted_iota(jnp.int32, (N_SMALL, 2 * dk), 0)
    ec = lax.broadcasted_iota(jnp.int32, (N_SMALL, 2 * dk), 1)
    sel_lane = jnp.where(ec < dk, LANE_B, LANE_A)
    tr = lax.broadcasted_iota(jnp.int32, (c, c), 0)
    tc = lax.broadcasted_iota(jnp.int32, (c, c), 1)
    lower = tc <= tr
    strict = tc < tr
    eye = jnp.where(tc == tr, 1.0, 0.0).astype(F32)

    def conv_silu(xs, cw_ref, lanes, r0):
        acc = None
        for j in range(CONV_K):
            lo = r0 + 8 - (CONV_K - 1) + j
            tap = xs[lo:lo + rb, :] * cw_ref[j:j + 1, lanes]
            acc = tap if acc is None else acc + tap
        return _silu(acc)

    def l2n(x):
        return x * lax.rsqrt(jnp.sum(x * x, axis=-1, keepdims=True) + NORM_EPS)

    for hp in range(GDN_HEADS_PER_STEP):
        lanes = slice(hp * dk, (hp + 1) * dk)
        head = pl.program_id(1) * GDN_HEADS_PER_STEP + hp

        for src, dst in ((q_ref, xq_s), (k_ref, xk_s), (v_ref, xv_s)):
            dst[0:8, :] = jnp.zeros((8, dk), F32)
            dst[8:8 + seq, :] = src[:, lanes].astype(F32)

        sel = jnp.where(er == sel_lane + head, 1.0, 0.0).astype(BF16)

        for i in range(seq // rb):
            r0 = i * rb
            rows = slice(r0, r0 + rb)
            q_s[rows, :] = l2n(conv_silu(xq_s, cwq_ref, lanes, r0)) * scale
            k_s[rows, :] = l2n(conv_silu(xk_s, cwk_ref, lanes, r0))
            v_s[rows, :] = conv_silu(xv_s, cwv_ref, lanes, r0)
            both = _dot_exact_rhs(sm_ref[rows, :], sel)
            bc_s[rows, :] = both[:, :dk]
            cc_s[rows, :] = both[:, dk:]

        for g0 in range(0, nchunk, GDN_PREP_GROUP):
            ns = list(range(g0, g0 + GDN_PREP_GROUP))
            rws = [slice(n * c, (n + 1) * c) for n in ns]
            qs = [q_s[r, :] for r in rws]
            ks = [k_s[r, :] for r in rws]
            vs = [v_s[r, :].astype(BF16) for r in rws]
            bcols = [bc_s[r, :] for r in rws]
            ccols = [cc_s[r, :] for r in rws]
            brows = [b.T for b in bcols]
            crows = [cc.T for cc in ccols]
            gams = [jnp.where(lower, jnp.exp(jnp.where(lower, cc - cr, 0.0)), 0.0)
                    for cc, cr in zip(ccols, crows)]
            kbs = [k.astype(BF16) for k in ks]
            kks = [_dot_nt(kb, kb) for kb in kbs]
            qks = [_dot_nt(q.astype(BF16), kb) for q, kb in zip(qs, kbs)]
            ms = [jnp.where(strict, b * kk * g, 0.0) for b, kk, g in zip(bcols, kks, gams)]
            tinvs = _unit_lower_inverse(ms, eye, tr, tc)
            tbs = [t * b for t, b in zip(tinvs, brows)]
            us = [_dot(tb.astype(BF16), v) for tb, v in zip(tbs, vs)]
            ws = [_dot((tb * jnp.exp(cr)).astype(BF16), kb).astype(BF16)
                  for tb, cr, kb in zip(tbs, crows, kbs)]
            lastrows = [cc[c - 1:c, :] for cc in ccols]
            kdts = [(k * jnp.exp(l - cc)).T.astype(BF16) for k, l, cc in zip(ks, lastrows, ccols)]
            wss = [_dot(kdt, w) for kdt, w in zip(kdts, ws)]
            bss = [_dot(kdt, u.astype(BF16)) for kdt, u in zip(kdts, us)]
            for j, n in enumerate(ns):
                u_s[hp, n] = us[j]
                wq_s[hp, n, 0:c, :] = ws[j]
                wq_s[hp, n, c:2 * c, :] = (qs[j] * jnp.exp(ccols[j])).astype(BF16)
                a_s[hp, n] = (qks[j] * gams[j]).astype(BF16)
                ws_s[hp, n] = wss[j].astype(BF16)
                bs_s[hp, n] = bss[j]
                gl_s[hp, n] = jnp.broadcast_to(jnp.exp(lastrows[j]), (8, GDN_DV))

    sts = [jnp.zeros((dk, GDN_DV), F32) for _ in range(GDN_HEADS_PER_STEP)]
    for n in range(nchunk):
        rows = slice(n * c, (n + 1) * c)
        for hp in range(GDN_HEADS_PER_STEP):
            lanes = slice(hp * dk, (hp + 1) * dk)
            sb = sts[hp].astype(BF16)
            sts[hp] = sts[hp] * gl_s[hp, n][0:1, :] - _dot(ws_s[hp, n], sb) + bs_s[hp, n]
            wq = _dot(wq_s[hp, n], sb)
            vb = (u_s[hp, n] - wq[0:c, :]).astype(BF16)
            o = wq[c:2 * c, :] + _dot(a_s[hp, n], vb)
            y = _gated_rmsnorm(o, nw_ref[...], z_ref[rows, lanes].astype(F32))
            y_ref[rows, lanes] = y.astype(y_ref.dtype)


def _gdn(u, small, cw8, norm_w, bsz, seq):
    t = u.shape[0]
    dk = GDN_DK
    hps = GDN_HEADS_PER_STEP
    wb = hps * dk
    nchunk = seq // GDN_CHUNK
    c = GDN_CHUNK
    conv_blocks = GDN_HEADS // hps
    return pl.pallas_call(
        functools.partial(_gdn_kernel, seq=seq),
        grid=(bsz, GDN_HEADS // hps),
        in_specs=[
            pl.BlockSpec((seq, wb), lambda b, p: (b, OFF_GDN_Q // wb + p)),
            pl.BlockSpec((seq, wb), lambda b, p: (b, OFF_GDN_K // wb + p)),
            pl.BlockSpec((seq, wb), lambda b, p: (b, OFF_GDN_V // wb + p)),
            pl.BlockSpec((seq, wb), lambda b, p: (b, OFF_GDN_Z // wb + p)),
            pl.BlockSpec((seq, N_SMALL), lambda b, p: (b, 0)),
            pl.BlockSpec((8, wb), lambda b, p: (0, p)),
            pl.BlockSpec((8, wb), lambda b, p: (0, conv_blocks + p)),
            pl.BlockSpec((8, wb), lambda b, p: (0, 2 * conv_blocks + p)),
            pl.BlockSpec((1, dk), lambda b, p: (0, 0)),
        ],
        out_specs=pl.BlockSpec((seq, wb), lambda b, p: (b, p)),
        out_shape=jax.ShapeDtypeStruct((t, GDN_HEADS * GDN_DV), BF16),
        scratch_shapes=[
            pltpu.VMEM((seq + 8, dk), F32),
            pltpu.VMEM((seq + 8, dk), F32),
            pltpu.VMEM((seq + 8, dk), F32),
            pltpu.VMEM((seq, dk), F32),
            pltpu.VMEM((seq, dk), F32),
            pltpu.VMEM((seq, dk), F32),
            pltpu.VMEM((seq, dk), F32),
            pltpu.VMEM((seq, dk), F32),
            pltpu.VMEM((hps, nchunk, c, dk), F32),
            pltpu.VMEM((hps, nchunk, 2 * c, dk), BF16),
            pltpu.VMEM((hps, nchunk, c, c), BF16),
            pltpu.VMEM((hps, nchunk, dk, dk), BF16),
            pltpu.VMEM((hps, nchunk, dk, dk), F32),
            pltpu.VMEM((hps, nchunk, 8, dk), F32),
        ],
        compiler_params=pltpu.CompilerParams(
            dimension_semantics=("arbitrary", "arbitrary"), vmem_limit_bytes=VMEM_LIMIT),
        name="gdn",
    )(u, u, u, u, small, cw8, cw8, cw8, norm_w)


def _merge_kernel(yg_ref, yd_ref, wa_ref, wb_ref, ga_ref, gb_ref, o_ref):
    pa = _dot(yg_ref[...], wa_ref[...])
    pb = _dot(yd_ref[...], wb_ref[...])
    ga = _sigmoid(ga_ref[...].astype(F32))
    gb = _sigmoid(gb_ref[...].astype(F32))
    o_ref[...] = (ga * pa + gb * pb).astype(o_ref.dtype)


def _merge(y_gla, y_gdn, w_branch, u):
    t, d = y_gla.shape
    tm, tn = 1024, 1024
    return pl.pallas_call(
        _merge_kernel,
        grid=(t // tm, d // tn),
        in_specs=[
            pl.BlockSpec((tm, d), lambda m, n: (m, 0)),
            pl.BlockSpec((tm, d), lambda m, n: (m, 0)),
            pl.BlockSpec((None, d, tn), lambda m, n: (0, 0, n)),
            pl.BlockSpec((None, d, tn), lambda m, n: (1, 0, n)),
            pl.BlockSpec((tm, tn), lambda m, n: (m, OFF_MERGE // tn + n)),
            pl.BlockSpec((tm, tn), lambda m, n: (m, (OFF_MERGE + D_MODEL) // tn + n)),
        ],
        out_specs=pl.BlockSpec((tm, tn), lambda m, n: (m, n)),
        out_shape=jax.ShapeDtypeStruct((t, d), BF16),
        compiler_params=pltpu.CompilerParams(
            dimension_semantics=("arbitrary", "arbitrary"), vmem_limit_bytes=VMEM_LIMIT),
        name="merge",
    )(y_gla, y_gdn, w_branch, w_branch, u, u)


def _outnorm_kernel(m_ref, w_ref, x_ref, gate_ref, lg_ref, lb_ref, o_ref, *, alpha):
    half = m_ref.shape[0] // 2
    outs = [_dot(m_ref[p * half:(p + 1) * half, :], w_ref[...]) for p in range(2)]
    for p, out in enumerate(outs):
        rows = slice(p * half, (p + 1) * half)
        r = alpha * x_ref[rows, :] + gate_ref[0] * out
        mu = jnp.mean(r, axis=-1, keepdims=True)
        rc = r - mu
        var = jnp.mean(rc * rc, axis=-1, keepdims=True)
        o_ref[rows, :] = rc * lax.rsqrt(var + LN_EPS) * lg_ref[...] + lb_ref[...]


def _outnorm(merged, w_out, x2, ada3, ln_g, ln_b, seq, alpha):
    t, d = x2.shape
    tm = 512
    per_b = seq // tm
    return pl.pallas_call(
        functools.partial(_outnorm_kernel, alpha=alpha),
        grid=(t // tm,),
        in_specs=[
            pl.BlockSpec((tm, d), lambda m: (m, 0)),
            pl.BlockSpec((d, d), lambda m: (0, 0)),
            pl.BlockSpec((tm, d), lambda m: (m, 0)),
            pl.BlockSpec((1, 1, d), lambda m: (m // per_b, 0, 2)),
            pl.BlockSpec((1, d), lambda m: (0, 0)),
            pl.BlockSpec((1, d), lambda m: (0, 0)),
        ],
        out_specs=pl.BlockSpec((tm, d), lambda m: (m, 0)),
        out_shape=jax.ShapeDtypeStruct((t, d), F32),
        compiler_params=pltpu.CompilerParams(
            dimension_semantics=("arbitrary",), vmem_limit_bytes=VMEM_LIMIT),
        name="outnorm",
    )(merged, w_out, x2, ada3, ln_g, ln_b)


def _pack_w_in(w_in):
    w_t = jnp.transpose(w_in).astype(BF16)
    glr = OFF_GDN_Q
    db_ = glr + GLA_RANK + 4 * GDN_HEADS * GDN_DK
    mg = db_ + 2 * GDN_HEADS
    assert mg + 2 * D_MODEL == w_in.shape[1]
    pad = jnp.zeros((N_SMALL - GLA_RANK - 2 * GDN_HEADS, w_in.shape[0]), BF16)
    small_t = jnp.concatenate([w_t[glr:glr + GLA_RANK], w_t[db_:mg], pad], axis=0)
    return w_t, small_t


def _layer(x, c, w_ada, b_ada, w_in, w_gk2, b_gk2, conv_w, a_log, dt_bias,
           gla_norm_w, gdn_norm_w, w_branch, w_out, ln_g, ln_b, alpha):
    bsz, seq, d = x.shape
    t = bsz * seq
    x2 = x.reshape(t, d)

    ada = _ada(c, w_ada, b_ada)
    ada3 = ada.reshape(bsz, 1, 3 * d)

    w_t, w_small_t = _pack_w_in(w_in)
    lane_pad = lambda v: jnp.zeros((1, N_SMALL), F32).at[0, LANE_A:LANE_A + GDN_HEADS].set(v)
    u, small = _inproj(x2, ada3, w_t, w_small_t, lane_pad(a_log), lane_pad(dt_bias), seq)

    w2p = jnp.concatenate(
        [w_gk2, jnp.zeros((N_SMALL - GLA_RANK, w_gk2.shape[1]), w_gk2.dtype)], axis=0).astype(BF16)
    y_gla = _gla(u, small, w2p, b_gk2.reshape(1, -1), gla_norm_w.reshape(1, -1), bsz, seq)

    cw8 = jnp.concatenate([conv_w, jnp.zeros((8 - CONV_K, conv_w.shape[1]), conv_w.dtype)], axis=0)
    y_gdn = _gdn(u, small, cw8, gdn_norm_w.reshape(1, -1), bsz, seq)

    merged = _merge(y_gla, y_gdn, w_branch.astype(BF16), u)
    out = _outnorm(merged, w_out.astype(BF16), x2, ada3, ln_g.reshape(1, -1), ln_b.reshape(1, -1),
                   seq, alpha)
    return out.reshape(bsz, seq, d)


def kernel(x, c, w_ada, b_ada, w_in, w_gk2, b_gk2, conv_w, a_log, dt_bias, gla_norm_w, gdn_norm_w,
           w_branch, w_out, ln_g, ln_b):
    depth = w_ada.shape[0]
    alpha = (2 * depth) ** 0.25
    for l in range(depth):
        x = _layer(x, c, w_ada[l], b_ada[l], w_in[l], w_gk2[l], b_gk2[l], conv_w[l], a_log[l],
                   dt_bias[l], gla_norm_w[l], gdn_norm_w[l], w_branch[l], w_out[l], ln_g[l], ln_b[l],
                   alpha)
    return x
```

```python
import functools

import jax
import jax.numpy as jnp
from jax import lax
from jax.experimental import pallas as pl
from jax.experimental.pallas import tpu as pltpu

F32 = jnp.float32
BF16 = jnp.bfloat16

D_MODEL = 2048
GLA_HEADS = 4
GLA_DK = 256
GLA_DV = 512
GLA_RANK = 16
GLA_GATE_NORM = 16.0
GDN_HEADS = 16
GDN_DK = 128
GDN_DV = 128
CONV_K = 4
NORM_EPS = 1e-6
LN_EPS = 1e-5

GLA_CHUNK = 64
GLA_GROUP = 8
GLA_PREP_BLOCKS = 4
GDN_CHUNK = 128
GDN_PREP_GROUP = 16
GDN_HEADS_PER_STEP = 2
INV_BASE = 8

OFF_GLA_Q = 0
OFF_GLA_K = 1024
OFF_GLA_V = 2048
OFF_GLA_G = 4096
OFF_GDN_Q = 6144
OFF_GDN_K = 8192
OFF_GDN_V = 10240
OFF_GDN_Z = 12288
OFF_MERGE = 14336
N_MAIN = 18432
N_SMALL = 128
LANE_B = 16
LANE_A = 32

ROW_BLOCK = 256
VMEM_LIMIT = 56 * 1024 * 1024


def _sigmoid(x):
    return 0.5 + 0.5 * jnp.tanh(0.5 * x)


def _silu(x):
    hx = 0.5 * x
    return hx + hx * jnp.tanh(hx)


def _softplus(x):
    return jnp.maximum(x, 0.0) + jnp.log1p(jnp.exp(-jnp.abs(x)))


def _dot(a, b):
    return jnp.dot(a, b, preferred_element_type=F32)


def _dot_nt(a, b):
    return lax.dot_general(a, b, (((1,), (1,)), ((), ())), preferred_element_type=F32)


def _dot_tn(a, b):
    return lax.dot_general(a, b, (((0,), (0,)), ((), ())), preferred_element_type=F32)


def _same_chunk(row, col, chunk):
    shift = chunk.bit_length() - 1
    return jnp.right_shift(row, shift) == jnp.right_shift(col, shift)


def _cumsum_matrix(n, chunk):
    row = lax.broadcasted_iota(jnp.int32, (n, n), 0)
    col = lax.broadcasted_iota(jnp.int32, (n, n), 1)
    return jnp.where(_same_chunk(row, col, chunk) & (col <= row), 1.0, 0.0).astype(BF16)


def _split3(x):
    hi = x.astype(BF16)
    r1 = x - hi.astype(F32)
    lo = r1.astype(BF16)
    lo2 = (r1 - lo.astype(F32)).astype(BF16)
    return hi, lo, lo2


def _dot_exact_lhs(m_bf16, x):
    hi, lo, lo2 = _split3(x)
    return _dot(m_bf16, hi) + _dot(m_bf16, lo) + _dot(m_bf16, lo2)


def _dot_exact_rhs(x, m_bf16):
    hi, lo, lo2 = _split3(x)
    return _dot(hi, m_bf16) + _dot(lo, m_bf16) + _dot(lo2, m_bf16)


def _gated_rmsnorm(o, norm_w, gate):
    ms = jnp.mean(o * o, axis=-1, keepdims=True)
    return o * lax.rsqrt(ms + NORM_EPS) * norm_w * _silu(gate)


def _ada_kernel(c_ref, w_ref, b_ref, o_ref):
    s = _silu(c_ref[...]).astype(BF16)
    o_ref[...] = _dot(s, w_ref[...].astype(BF16)) + b_ref[...]


def _ada(c, w_ada, b_ada):
    bsz, d = c.shape
    n = w_ada.shape[1]
    tn = 768
    return pl.pallas_call(
        _ada_kernel,
        grid=(n // tn,),
        in_specs=[
            pl.BlockSpec((bsz, d), lambda j: (0, 0)),
            pl.BlockSpec((d, tn), lambda j: (0, j)),
            pl.BlockSpec((1, tn), lambda j: (0, j)),
        ],
        out_specs=pl.BlockSpec((bsz, tn), lambda j: (0, j)),
        out_shape=jax.ShapeDtypeStruct((bsz, n), F32),
        compiler_params=pltpu.CompilerParams(
            dimension_semantics=("arbitrary",), vmem_limit_bytes=VMEM_LIMIT),
        name="ada",
    )(c, w_ada, b_ada.reshape(1, n))


def _inproj_kernel(x_ref, sc_ref, sh_ref, wt_ref, wst_ref, al_ref, dt_ref, u_ref, sm_ref, h_scr, *, tm):
    @pl.when(pl.program_id(1) == 0)
    def _():
        sc = 1.0 + sc_ref[0]
        sh = sh_ref[0]
        rb = ROW_BLOCK
        for r in range(tm // rb):
            h = x_ref[r * rb:(r + 1) * rb, :] * sc + sh
            h_scr[r * rb:(r + 1) * rb, :] = h.astype(BF16)
        l_incl = _cumsum_matrix(rb, GDN_CHUNK)
        neg_a = -jnp.exp(al_ref[...])
        lane = lax.broadcasted_iota(jnp.int32, (rb, N_SMALL), 1)
        raw = _dot_nt(h_scr[...], wst_ref[...])
        g = neg_a * _softplus(raw + dt_ref[...])
        parts = [_split3(g[r * rb:(r + 1) * rb, :]) for r in range(tm // rb)]
        cums = [_dot(l_incl, hi) + _dot(l_incl, lo) + _dot(l_incl, lo2) for hi, lo, lo2 in parts]
        for r, cum in enumerate(cums):
            raw_r = raw[r * rb:(r + 1) * rb, :]
            sm_ref[r * rb:(r + 1) * rb, :] = jnp.where(
                lane < LANE_B, raw_r, jnp.where(lane < LANE_A, _sigmoid(raw_r), cum))

    u_ref[...] = _dot_nt(h_scr[...], wt_ref[...]).astype(u_ref.dtype)


def _inproj(x2, ada3, w_t, w_small_t, alog_row, dt_row, seq):
    t, d = x2.shape
    tm, tn = 1024, 2048
    per_b = seq // tm
    nb_a = OFF_GDN_Q // tn
    nb_b = (OFF_MERGE - OFF_GDN_Q) // tn

    def w_row(n):
        skip = jnp.where(n < nb_a, 0, jnp.where(n < nb_a + nb_b, GLA_RANK, GLA_RANK + 2 * GDN_HEADS))
        return pl.multiple_of(n * tn + skip, 16)

    return pl.pallas_call(
        functools.partial(_inproj_kernel, tm=tm),
        grid=(t // tm, N_MAIN // tn),
        in_specs=[
            pl.BlockSpec((tm, d), lambda m, n: (m, 0)),
            pl.BlockSpec((1, 1, d), lambda m, n: (m // per_b, 0, 1)),
            pl.BlockSpec((1, 1, d), lambda m, n: (m // per_b, 0, 0)),
            pl.BlockSpec((pl.Element(tn), pl.Element(d)), lambda m, n: (w_row(n), 0)),
            pl.BlockSpec((N_SMALL, d), lambda m, n: (0, 0)),
            pl.BlockSpec((1, N_SMALL), lambda m, n: (0, 0)),
            pl.BlockSpec((1, N_SMALL), lambda m, n: (0, 0)),
        ],
        out_specs=[
            pl.BlockSpec((tm, tn), lambda m, n: (m, n)),
            pl.BlockSpec((tm, N_SMALL), lambda m, n: (m, 0)),
        ],
        out_shape=[
            jax.ShapeDtypeStruct((t, N_MAIN), BF16),
            jax.ShapeDtypeStruct((t, N_SMALL), F32),
        ],
        scratch_shapes=[pltpu.VMEM((tm, d), BF16)],
        compiler_params=pltpu.CompilerParams(
            dimension_semantics=("arbitrary", "arbitrary"), vmem_limit_bytes=VMEM_LIMIT),
        name="inproj",
    )(x2, ada3, ada3, w_t, w_small_t, alog_row, dt_row)


def _gla_kernel(q_ref, k_ref, v_ref, g_ref, sm_ref, w2_ref, b2_ref, nw_ref, y_ref,
                qe_s, ki_s, kd_s, dl_s, st_s, *, seq):
    c = GLA_CHUNK
    rb = ROW_BLOCK
    cpb = rb // c
    l_incl = _cumsum_matrix(rb, c)
    scale = GLA_DK ** -0.5

    pg = GLA_PREP_BLOCKS

    def prep(i, carry):
        blks = [i * pg + b for b in range(pg)]
        rws = [pl.ds(pl.multiple_of(blk * rb, rb), rb) for blk in blks]
        zs = [_dot(sm_ref[r, :].astype(BF16), w2_ref[...]) + b2_ref[...] for r in rws]
        las = [-_softplus(-z) * (1.0 / GLA_GATE_NORM) for z in zs]
        his = [la.astype(BF16) for la in las]
        los = [(la - hi.astype(F32)).astype(BF16) for la, hi in zip(las, his)]
        cums = [_dot(l_incl, hi) + _dot(l_incl, lo) for hi, lo in zip(his, los)]
        for blk, r, cum in zip(blks, rws, cums):
            lasts = [cum[j * c + c - 1:j * c + c, :] for j in range(cpb)]
            elast = jnp.concatenate([jnp.broadcast_to(jnp.exp(l), (c, GLA_DK)) for l in lasts], axis=0)
            q = q_ref[r, :].astype(F32)
            k = k_ref[r, :].astype(F32)
            ki = k * jnp.exp(-cum)
            qe_s[r, :] = (q * (jnp.exp(cum) * scale)).astype(BF16)
            ki_s[r, :] = ki.astype(BF16)
            kd_s[r, :] = (ki * elast).astype(BF16)
            for j in range(cpb):
                dl_s[blk * cpb + j] = elast[j * c:j * c + 8, :]
        return carry

    lax.fori_loop(0, seq // (rb * pg), prep, 0)

    st_s[...] = jnp.zeros_like(st_s)
    tr = lax.broadcasted_iota(jnp.int32, (c, c), 0)
    tc = lax.broadcasted_iota(jnp.int32, (c, c), 1)
    causal = tc <= tr
    grp = GLA_GROUP

    def step(i, carry):
        ns = [i * grp + j for j in range(grp)]
        ras = [pl.ds(pl.multiple_of(n * 2 * c, c), c) for n in ns]
        rbs = [pl.ds(pl.multiple_of(n * 2 * c + c, c), c) for n in ns]
        rabs = [pl.ds(pl.multiple_of(n * 2 * c, 2 * c), 2 * c) for n in ns]
        das = [dl_s[2 * n][0:1, :] for n in ns]
        dbs = [dl_s[2 * n + 1][0:1, :] for n in ns]
        qas = [qe_s[r, :] for r in ras]
        qbs = [qe_s[r, :] for r in rbs]
        kdas = [kd_s[r, :] for r in ras]
        v2s = [v_ref[r, :] for r in rabs]
        aas = [jnp.where(causal, _dot_nt(qa, ki_s[r, :]), 0.0).astype(BF16) for qa, r in zip(qas, ras)]
        bbs = [jnp.where(causal, _dot_nt(qb, ki_s[r, :]), 0.0).astype(BF16) for qb, r in zip(qbs, rbs)]
        bas = [_dot_nt(qb, kda).astype(BF16) for qb, kda in zip(qbs, kdas)]
        babs = [jnp.concatenate([ba, bb], axis=1) for ba, bb in zip(bas, bbs)]
        q2s = [jnp.concatenate([qa, (qb.astype(F32) * da).astype(BF16)], axis=0)
               for qa, qb, da in zip(qas, qbs, das)]
        kd2s = [jnp.concatenate([(kda.astype(F32) * db).astype(BF16), kd_s[r, :]], axis=0)
                for kda, db, r in zip(kdas, dbs, rbs)]
        upds = [_dot_tn(v2, kd2) for v2, kd2 in zip(v2s, kd2s)]
        st = st_s[...]
        sts = []
        for j in range(grp):
            sts.append(st.astype(BF16))
            st = st * (das[j] * dbs[j]) + upds[j]
        st_s[...] = st
        for j in range(grp):
            o2 = _dot_nt(q2s[j], sts[j])
            oa = o2[0:c, :] + _dot(aas[j], v2s[j][0:c, :])
            ob = o2[c:2 * c, :] + _dot(babs[j], v2s[j])
            ya = _gated_rmsnorm(oa, nw_ref[...], g_ref[ras[j], :].astype(F32))
            yb = _gated_rmsnorm(ob, nw_ref[...], g_ref[rbs[j], :].astype(F32))
            y_ref[ras[j], :] = ya.astype(y_ref.dtype)
            y_ref[rbs[j], :] = yb.astype(y_ref.dtype)
        return carry

    lax.fori_loop(0, seq // (2 * c * grp), step, 0)


def _gla(u, small, w2p, b2, norm_w, bsz, seq):
    t = u.shape[0]
    qb, vb = GLA_DK, GLA_DV
    return pl.pallas_call(
        functools.partial(_gla_kernel, seq=seq),
        grid=(bsz, GLA_HEADS),
        in_specs=[
            pl.BlockSpec((seq, qb), lambda b, h: (b, OFF_GLA_Q // qb + h)),
            pl.BlockSpec((seq, qb), lambda b, h: (b, OFF_GLA_K // qb + h)),
            pl.BlockSpec((seq, vb), lambda b, h: (b, OFF_GLA_V // vb + h)),
            pl.BlockSpec((seq, vb), lambda b, h: (b, OFF_GLA_G // vb + h)),
            pl.BlockSpec((seq, N_SMALL), lambda b, h: (b, 0)),
            pl.BlockSpec((N_SMALL, qb), lambda b, h: (0, h)),
            pl.BlockSpec((1, qb), lambda b, h: (0, h)),
            pl.BlockSpec((1, vb), lambda b, h: (0, 0)),
        ],
        out_specs=pl.BlockSpec((seq, vb), lambda b, h: (b, h)),
        out_shape=jax.ShapeDtypeStruct((t, GLA_HEADS * GLA_DV), BF16),
        scratch_shapes=[
            pltpu.VMEM((seq, qb), BF16),
            pltpu.VMEM((seq, qb), BF16),
            pltpu.VMEM((seq, qb), BF16),
            pltpu.VMEM((seq // GLA_CHUNK, 8, qb), F32),
            pltpu.VMEM((vb, qb), F32),
        ],
        compiler_params=pltpu.CompilerParams(
            dimension_semantics=("arbitrary", "arbitrary"), vmem_limit_bytes=VMEM_LIMIT),
        name="gla",
    )(u, u, u, u, small, w2p, b2, norm_w)


def _unit_lower_inverse(ms, eye, tr, tc):
    n = ms[0].shape[0]
    base = _same_chunk(tr, tc, INV_BASE)
    ps = [jnp.where(base, -m, 0.0) for m in ms]
    tinvs = [eye + p for p in ps]
    size = 2
    while size < INV_BASE:
        pbs = [p.astype(BF16) for p in ps]
        ps = [_dot(pb, pb) for pb in pbs]
        tinvs = [t + _dot(t.astype(BF16), p.astype(BF16)) for t, p in zip(tinvs, ps)]
        size *= 2
    blk = INV_BASE
    while blk < n:
        off = _same_chunk(tr, tc, 2 * blk) & jnp.logical_not(_same_chunk(tr, tc, blk))
        moffs = [jnp.where(off, m, 0.0).astype(BF16) for m in ms]
        tbs = [t.astype(BF16) for t in tinvs]
        mids = [_dot(tb, mo).astype(BF16) for tb, mo in zip(tbs, moffs)]
        tinvs = [t - _dot(mid, tb) for t, mid, tb in zip(tinvs, mids, tbs)]
        blk *= 2
    return tinvs


def _gdn_kernel(q_ref, k_ref, v_ref, z_ref, sm_ref, cwq_ref, cwk_ref, cwv_ref, nw_ref, y_ref,
                xq_s, xk_s, xv_s, q_s, k_s, v_s, bc_s, cc_s,
                u_s, wq_s, a_s, ws_s, bs_s, gl_s, *, seq):
    c = GDN_CHUNK
    nchunk = seq // c
    rb = ROW_BLOCK
    dk = GDN_DK
    scale = GDN_DK ** -0.5

    er = lax.broadcasted_iota(jnp.int32, (N_SMALL, 2 * dk), 0)
    ec = lax.broadcasted_iota(jnp.int32, (N_SMALL, 2 * dk), 1)
    sel_lane = jnp.where(ec < dk, LANE_B, LANE_A)
    tr = lax.broadcasted_iota(jnp.int32, (c, c), 0)
    tc = lax.broadcasted_iota(jnp.int32, (c, c), 1)
    lower = tc <= tr
    strict = tc < tr
    eye = jnp.where(tc == tr, 1.0, 0.0).astype(F32)

    def conv_silu(xs, cw_ref, lanes, r0):
        acc = None
        for j in range(CONV_K):
            lo = r0 + 8 - (CONV_K - 1) + j
            tap = xs[lo:lo + rb, :] * cw_ref[j:j + 1, lanes]
            acc = tap if acc is None else acc + tap
        return _silu(acc)

    def l2n(x):
        return x * lax.rsqrt(jnp.sum(x * x, axis=-1, keepdims=True) + NORM_EPS)

    for hp in range(GDN_HEADS_PER_STEP):
        lanes = slice(hp * dk, (hp + 1) * dk)
        head = pl.program_id(1) * GDN_HEADS_PER_STEP + hp

        for src, dst in ((q_ref, xq_s), (k_ref, xk_s), (v_ref, xv_s)):
            dst[0:8, :] = jnp.zeros((8, dk), F32)
            dst[8:8 + seq, :] = src[:, lanes].astype(F32)

        sel = jnp.where(er == sel_lane + head, 1.0, 0.0).astype(BF16)

        for i in range(seq // rb):
            r0 = i * rb
            rows = slice(r0, r0 + rb)
            q_s[rows, :] = l2n(conv_silu(xq_s, cwq_ref, lanes, r0)) * scale
            k_s[rows, :] = l2n(conv_silu(xk_s, cwk_ref, lanes, r0))
            v_s[rows, :] = conv_silu(xv_s, cwv_ref, lanes, r0)
            both = _dot_exact_rhs(sm_ref[rows, :], sel)
            bc_s[rows, :] = both[:, :dk]
            cc_s[rows, :] = both[:, dk:]

        for g0 in range(0, nchunk, GDN_PREP_GROUP):
            ns = list(range(g0, g0 + GDN_PREP_GROUP))
            rws = [slice(n * c, (n + 1) * c) for n in ns]
            qs = [q_s[r, :] for r in rws]
            ks = [k_s[r, :] for r in rws]
            vs = [v_s[r, :].astype(BF16) for r in rws]
            bcols = [bc_s[r, :] for r in rws]
            ccols = [cc_s[r, :] for r in rws]
            brows = [b.T for b in bcols]
            crows = [cc.T for cc in ccols]
            gams = [jnp.where(lower, jnp.exp(jnp.where(lower, cc - cr, 0.0)), 0.0)
                    for cc, cr in zip(ccols, crows)]
            kbs = [k.astype(BF16) for k in ks]
            kks = [_dot_nt(kb, kb) for kb in kbs]
            qks = [_dot_nt(q.astype(BF16), kb) for q, kb in zip(qs, kbs)]
            ms = [jnp.where(strict, b * kk * g, 0.0) for b, kk, g in zip(bcols, kks, gams)]
            tinvs = _unit_lower_inverse(ms, eye, tr, tc)
            tbs = [t * b for t, b in zip(tinvs, brows)]
            us = [_dot(tb.astype(BF16), v) for tb, v in zip(tbs, vs)]
            ws = [_dot((tb * jnp.exp(cr)).astype(BF16), kb).astype(BF16)
                  for tb, cr, kb in zip(tbs, crows, kbs)]
            lastrows = [cc[c - 1:c, :] for cc in ccols]
            kdts = [(k * jnp.exp(l - cc)).T.astype(BF16) for k, l, cc in zip(ks, lastrows, ccols)]
            wss = [_dot(kdt, w) for kdt, w in zip(kdts, ws)]
            bss = [_dot(kdt, u.astype(BF16)) for kdt, u in zip(kdts, us)]
            for j, n in enumerate(ns):
                u_s[hp, n] = us[j]
                wq_s[hp, n, 0:c, :] = ws[j]
                wq_s[hp, n, c:2 * c, :] = (qs[j] * jnp.exp(ccols[j])).astype(BF16)
                a_s[hp, n] = (qks[j] * gams[j]).astype(BF16)
                ws_s[hp, n] = wss[j].astype(BF16)
                bs_s[hp, n] = bss[j]
                gl_s[hp, n] = jnp.broadcast_to(jnp.exp(lastrows[j]), (8, GDN_DV))

    sts = [jnp.zeros((dk, GDN_DV), F32) for _ in range(GDN_HEADS_PER_STEP)]
    for n in range(nchunk):
        rows = slice(n * c, (n + 1) * c)
        for hp in range(GDN_HEADS_PER_STEP):
            lanes = slice(hp * dk, (hp + 1) * dk)
            sb = sts[hp].astype(BF16)
            sts[hp] = sts[hp] * gl_s[hp, n][0:1, :] - _dot(ws_s[hp, n], sb) + bs_s[hp, n]
            wq = _dot(wq_s[hp, n], sb)
            vb = (u_s[hp, n] - wq[0:c, :]).astype(BF16)
            o = wq[c:2 * c, :] + _dot(a_s[hp, n], vb)
            y = _gated_rmsnorm(o, nw_ref[...], z_ref[rows, lanes].astype(F32))
            y_ref[rows, lanes] = y.astype(y_ref.dtype)


def _gdn(u, small, cw8, norm_w, bsz, seq):
    t = u.shape[0]
    dk = GDN_DK
    hps = GDN_HEADS_PER_STEP
    wb = hps * dk
    nchunk = seq // GDN_CHUNK
    c = GDN_CHUNK
    conv_blocks = GDN_HEADS // hps
    return pl.pallas_call(
        functools.partial(_gdn_kernel, seq=seq),
        grid=(bsz, GDN_HEADS // hps),
        in_specs=[
            pl.BlockSpec((seq, wb), lambda b, p: (b, OFF_GDN_Q // wb + p)),
            pl.BlockSpec((seq, wb), lambda b, p: (b, OFF_GDN_K // wb + p)),
            pl.BlockSpec((seq, wb), lambda b, p: (b, OFF_GDN_V // wb + p)),
            pl.BlockSpec((seq, wb), lambda b, p: (b, OFF_GDN_Z // wb + p)),
            pl.BlockSpec((seq, N_SMALL), lambda b, p: (b, 0)),
            pl.BlockSpec((8, wb), lambda b, p: (0, p)),
            pl.BlockSpec((8, wb), lambda b, p: (0, conv_blocks + p)),
            pl.BlockSpec((8, wb), lambda b, p: (0, 2 * conv_blocks + p)),
            pl.BlockSpec((1, dk), lambda b, p: (0, 0)),
        ],
        out_specs=pl.BlockSpec((seq, wb), lambda b, p: (b, p)),
        out_shape=jax.ShapeDtypeStruct((t, GDN_HEADS * GDN_DV), BF16),
        scratch_shapes=[
            pltpu.VMEM((seq + 8, dk), F32),
            pltpu.VMEM((seq + 8, dk), F32),
            pltpu.VMEM((seq + 8, dk), F32),
            pltpu.VMEM((seq, dk), F32),
            pltpu.VMEM((seq, dk), F32),
            pltpu.VMEM((seq, dk), F32),
            pltpu.VMEM((seq, dk), F32),
            pltpu.VMEM((seq, dk), F32),
            pltpu.VMEM((hps, nchunk, c, dk), F32),
            pltpu.VMEM((hps, nchunk, 2 * c, dk), BF16),
            pltpu.VMEM((hps, nchunk, c, c), BF16),
            pltpu.VMEM((hps, nchunk, dk, dk), BF16),
            pltpu.VMEM((hps, nchunk, dk, dk), F32),
            pltpu.VMEM((hps, nchunk, 8, dk), F32),
        ],
        compiler_params=pltpu.CompilerParams(
            dimension_semantics=("arbitrary", "arbitrary"), vmem_limit_bytes=VMEM_LIMIT),
        name="gdn",
    )(u, u, u, u, small, cw8, cw8, cw8, norm_w)


def _merge_kernel(yg_ref, yd_ref, wa_ref, wb_ref, ga_ref, gb_ref, o_ref):
    pa = _dot(yg_ref[...], wa_ref[...])
    pb = _dot(yd_ref[...], wb_ref[...])
    ga = _sigmoid(ga_ref[...].astype(F32))
    gb = _sigmoid(gb_ref[...].astype(F32))
    o_ref[...] = (ga * pa + gb * pb).astype(o_ref.dtype)


def _merge(y_gla, y_gdn, w_branch, u):
    t, d = y_gla.shape
    tm, tn = 1024, 1024
    return pl.pallas_call(
        _merge_kernel,
        grid=(t // tm, d // tn),
        in_specs=[
            pl.BlockSpec((tm, d), lambda m, n: (m, 0)),
            pl.BlockSpec((tm, d), lambda m, n: (m, 0)),
            pl.BlockSpec((None, d, tn), lambda m, n: (0, 0, n)),
            pl.BlockSpec((None, d, tn), lambda m, n: (1, 0, n)),
            pl.BlockSpec((tm, tn), lambda m, n: (m, OFF_MERGE // tn + n)),
            pl.BlockSpec((tm, tn), lambda m, n: (m, (OFF_MERGE + D_MODEL) // tn + n)),
        ],
        out_specs=pl.BlockSpec((tm, tn), lambda m, n: (m, n)),
        out_shape=jax.ShapeDtypeStruct((t, d), BF16),
        compiler_params=pltpu.CompilerParams(
            dimension_semantics=("arbitrary", "arbitrary"), vmem_limit_bytes=VMEM_LIMIT),
        name="merge",
    )(y_gla, y_gdn, w_branch, w_branch, u, u)


def _outnorm_kernel(m_ref, w_ref, x_ref, gate_ref, lg_ref, lb_ref, o_ref, *, alpha):
    half = m_ref.shape[0] // 2
    outs = [_dot(m_ref[p * half:(p + 1) * half, :], w_ref[...]) for p in range(2)]
    for p, out in enumerate(outs):
        rows = slice(p * half, (p + 1) * half)
        r = alpha * x_ref[rows, :] + gate_ref[0] * out
        mu = jnp.mean(r, axis=-1, keepdims=True)
        rc = r - mu
        var = jnp.mean(rc * rc, axis=-1, keepdims=True)
        o_ref[rows, :] = rc * lax.rsqrt(var + LN_EPS) * lg_ref[...] + lb_ref[...]


def _outnorm(merged, w_out, x2, ada3, ln_g, ln_b, seq, alpha):
    t, d = x2.shape
    tm = 512
    per_b = seq // tm
    return pl.pallas_call(
        functools.partial(_outnorm_kernel, alpha=alpha),
        grid=(t // tm,),
        in_specs=[
            pl.BlockSpec((tm, d), lambda m: (m, 0)),
            pl.BlockSpec((d, d), lambda m: (0, 0)),
            pl.BlockSpec((tm, d), lambda m: (m, 0)),
            pl.BlockSpec((1, 1, d), lambda m: (m // per_b, 0, 2)),
            pl.BlockSpec((1, d), lambda m: (0, 0)),
            pl.BlockSpec((1, d), lambda m: (0, 0)),
        ],
        out_specs=pl.BlockSpec((tm, d), lambda m: (m, 0)),
        out_shape=jax.ShapeDtypeStruct((t, d), F32),
        compiler_params=pltpu.CompilerParams(
            dimension_semantics=("arbitrary",), vmem_limit_bytes=VMEM_LIMIT),
        name="outnorm",
    )(merged, w_out, x2, ada3, ln_g, ln_b)


def _pack_w_in(w_in):
    w_t = jnp.transpose(w_in).astype(BF16)
    glr = OFF_GDN_Q
    db_ = glr + GLA_RANK + 4 * GDN_HEADS * GDN_DK
    mg = db_ + 2 * GDN_HEADS
    assert mg + 2 * D_MODEL == w_in.shape[1]
    pad = jnp.zeros((N_SMALL - GLA_RANK - 2 * GDN_HEADS, w_in.shape[0]), BF16)
    small_t = jnp.concatenate([w_t[glr:glr + GLA_RANK], w_t[db_:mg], pad], axis=0)
    return w_t, small_t


def _layer(x, c, w_ada, b_ada, w_in, w_gk2, b_gk2, conv_w, a_log, dt_bias,
           gla_norm_w, gdn_norm_w, w_branch, w_out, ln_g, ln_b, alpha):
    bsz, seq, d = x.shape
    t = bsz * seq
    x2 = x.reshape(t, d)

    ada = _ada(c, w_ada, b_ada)
    ada3 = ada.reshape(bsz, 1, 3 * d)

    w_t, w_small_t = _pack_w_in(w_in)
    lane_pad = lambda v: jnp.zeros((1, N_SMALL), F32).at[0, LANE_A:LANE_A + GDN_HEADS].set(v)
    u, small = _inproj(x2, ada3, w_t, w_small_t, lane_pad(a_log), lane_pad(dt_bias), seq)

    w2p = jnp.concatenate(
        [w_gk2, jnp.zeros((N_SMALL - GLA_RANK, w_gk2.shape[1]), w_gk2.dtype)], axis=0).astype(BF16)
    y_gla = _gla(u, small, w2p, b_gk2.reshape(1, -1), gla_norm_w.reshape(1, -1), bsz, seq)

    cw8 = jnp.concatenate([conv_w, jnp.zeros((8 - CONV_K, conv_w.shape[1]), conv_w.dtype)], axis=0)
    y_gdn = _gdn(u, small, cw8, gdn_norm_w.reshape(1, -1), bsz, seq)

    merged = _merge(y_gla, y_gdn, w_branch.astype(BF16), u)
    out = _outnorm(merged, w_out.astype(BF16), x2, ada3, ln_g.reshape(1, -1), ln_b.reshape(1, -1),
                   seq, alpha)
    return out.reshape(bsz, seq, d)


def kernel(x, c, w_ada, b_ada, w_in, w_gk2, b_gk2, conv_w, a_log, dt_bias, gla_norm_w, gdn_norm_w,
           w_branch, w_out, ln_g, ln_b):
    depth = w_ada.shape[0]
    alpha = (2 * depth) ** 0.25
    for l in range(depth):
        x = _layer(x, c, w_ada[l], b_ada[l], w_in[l], w_gk2[l], b_gk2[l], conv_w[l], a_log[l],
                   dt_bias[l], gla_norm_w[l], gdn_norm_w[l], w_branch[l], w_out[l], ln_g[l], ln_b[l],
                   alpha)
    return x
```

```python
import functools

import jax
import jax.numpy as jnp
from jax import lax
from jax.experimental import pallas as pl
from jax.experimental.pallas import tpu as pltpu

F32 = jnp.float32
BF16 = jnp.bfloat16

D_MODEL = 2048
GLA_HEADS = 4
GLA_DK = 256
GLA_DV = 512
GLA_RANK = 16
GLA_GATE_NORM = 16.0
GDN_HEADS = 16
GDN_DK = 128
GDN_DV = 128
CONV_K = 4
NORM_EPS = 1e-6
LN_EPS = 1e-5

GLA_CHUNK = 64
GLA_GROUP = 8
GLA_PREP_BLOCKS = 4
GDN_CHUNK = 128
GDN_PREP_GROUP = 16
GDN_HEADS_PER_STEP = 4
INV_BASE = 8

OFF_GLA_Q = 0
OFF_GLA_K = 1024
OFF_GLA_V = 2048
OFF_GLA_G = 4096
OFF_GDN_Q = 6144
OFF_GDN_K = 8192
OFF_GDN_V = 10240
OFF_GDN_Z = 12288
OFF_MERGE = 14336
N_MAIN = 18432
N_SMALL = 128
LANE_B = 16
LANE_A = 32

ROW_BLOCK = 256
VMEM_LIMIT = 56 * 1024 * 1024


def _sigmoid(x):
    return 0.5 + 0.5 * jnp.tanh(0.5 * x)


def _silu(x):
    hx = 0.5 * x
    return hx + hx * jnp.tanh(hx)


def _softplus(x):
    return jnp.maximum(x, 0.0) + jnp.log1p(jnp.exp(-jnp.abs(x)))


def _dot(a, b):
    return jnp.dot(a, b, preferred_element_type=F32)


def _dot_nt(a, b):
    return lax.dot_general(a, b, (((1,), (1,)), ((), ())), preferred_element_type=F32)


def _dot_tn(a, b):
    return lax.dot_general(a, b, (((0,), (0,)), ((), ())), preferred_element_type=F32)


def _same_chunk(row, col, chunk):
    shift = chunk.bit_length() - 1
    return jnp.right_shift(row, shift) == jnp.right_shift(col, shift)


def _cumsum_matrix(n, chunk):
    row = lax.broadcasted_iota(jnp.int32, (n, n), 0)
    col = lax.broadcasted_iota(jnp.int32, (n, n), 1)
    return jnp.where(_same_chunk(row, col, chunk) & (col <= row), 1.0, 0.0).astype(BF16)


def _split3(x):
    hi = x.astype(BF16)
    r1 = x - hi.astype(F32)
    lo = r1.astype(BF16)
    lo2 = (r1 - lo.astype(F32)).astype(BF16)
    return hi, lo, lo2


def _dot_exact_lhs(m_bf16, x):
    hi, lo, lo2 = _split3(x)
    return _dot(m_bf16, hi) + _dot(m_bf16, lo) + _dot(m_bf16, lo2)


def _dot_exact_rhs(x, m_bf16):
    hi, lo, lo2 = _split3(x)
    return _dot(hi, m_bf16) + _dot(lo, m_bf16) + _dot(lo2, m_bf16)


def _gated_rmsnorm(o, norm_w, gate):
    ms = jnp.mean(o * o, axis=-1, keepdims=True)
    return o * lax.rsqrt(ms + NORM_EPS) * norm_w * _silu(gate)


def _ada_kernel(c_ref, w_ref, b_ref, o_ref):
    s = _silu(c_ref[...]).astype(BF16)
    o_ref[...] = _dot(s, w_ref[...].astype(BF16)) + b_ref[...]


def _ada(c, w_ada, b_ada):
    bsz, d = c.shape
    n = w_ada.shape[1]
    tn = 768
    return pl.pallas_call(
        _ada_kernel,
        grid=(n // tn,),
        in_specs=[
            pl.BlockSpec((bsz, d), lambda j: (0, 0)),
            pl.BlockSpec((d, tn), lambda j: (0, j)),
            pl.BlockSpec((1, tn), lambda j: (0, j)),
        ],
        out_specs=pl.BlockSpec((bsz, tn), lambda j: (0, j)),
        out_shape=jax.ShapeDtypeStruct((bsz, n), F32),
        compiler_params=pltpu.CompilerParams(
            dimension_semantics=("arbitrary",), vmem_limit_bytes=VMEM_LIMIT),
        name="ada",
    )(c, w_ada, b_ada.reshape(1, n))


def _inproj_kernel(x_ref, sc_ref, sh_ref, wt_ref, wst_ref, al_ref, dt_ref, u_ref, sm_ref, h_scr, *, tm):
    @pl.when(pl.program_id(1) == 0)
    def _():
        sc = 1.0 + sc_ref[0]
        sh = sh_ref[0]
        rb = ROW_BLOCK
        for r in range(tm // rb):
            h = x_ref[r * rb:(r + 1) * rb, :] * sc + sh
            h_scr[r * rb:(r + 1) * rb, :] = h.astype(BF16)
        l_incl = _cumsum_matrix(rb, GDN_CHUNK)
        neg_a = -jnp.exp(al_ref[...])
        lane = lax.broadcasted_iota(jnp.int32, (rb, N_SMALL), 1)
        raw = _dot_nt(h_scr[...], wst_ref[...])
        g = neg_a * _softplus(raw + dt_ref[...])
        parts = [_split3(g[r * rb:(r + 1) * rb, :]) for r in range(tm // rb)]
        cums = [_dot(l_incl, hi) + _dot(l_incl, lo) + _dot(l_incl, lo2) for hi, lo, lo2 in parts]
        for r, cum in enumerate(cums):
            raw_r = raw[r * rb:(r + 1) * rb, :]
            sm_ref[r * rb:(r + 1) * rb, :] = jnp.where(
                lane < LANE_B, raw_r, jnp.where(lane < LANE_A, _sigmoid(raw_r), cum))

    u_ref[...] = _dot_nt(h_scr[...], wt_ref[...]).astype(u_ref.dtype)


def _inproj(x2, ada3, w_t, w_small_t, alog_row, dt_row, seq):
    t, d = x2.shape
    tm, tn = 1024, 2048
    per_b = seq // tm
    nb_a = OFF_GDN_Q // tn
    nb_b = (OFF_MERGE - OFF_GDN_Q) // tn

    def w_row(n):
        skip = jnp.where(n < nb_a, 0, jnp.where(n < nb_a + nb_b, GLA_RANK, GLA_RANK + 2 * GDN_HEADS))
        return pl.multiple_of(n * tn + skip, 16)

    return pl.pallas_call(
        functools.partial(_inproj_kernel, tm=tm),
        grid=(t // tm, N_MAIN // tn),
        in_specs=[
            pl.BlockSpec((tm, d), lambda m, n: (m, 0)),
            pl.BlockSpec((1, 1, d), lambda m, n: (m // per_b, 0, 1)),
            pl.BlockSpec((1, 1, d), lambda m, n: (m // per_b, 0, 0)),
            pl.BlockSpec((pl.Element(tn), pl.Element(d)), lambda m, n: (w_row(n), 0)),
            pl.BlockSpec((N_SMALL, d), lambda m, n: (0, 0)),
            pl.BlockSpec((1, N_SMALL), lambda m, n: (0, 0)),
            pl.BlockSpec((1, N_SMALL), lambda m, n: (0, 0)),
        ],
        out_specs=[
            pl.BlockSpec((tm, tn), lambda m, n: (m, n)),
            pl.BlockSpec((tm, N_SMALL), lambda m, n: (m, 0)),
        ],
        out_shape=[
            jax.ShapeDtypeStruct((t, N_MAIN), BF16),
            jax.ShapeDtypeStruct((t, N_SMALL), F32),
        ],
        scratch_shapes=[pltpu.VMEM((tm, d), BF16)],
        compiler_params=pltpu.CompilerParams(
            dimension_semantics=("arbitrary", "arbitrary"), vmem_limit_bytes=VMEM_LIMIT),
        name="inproj",
    )(x2, ada3, ada3, w_t, w_small_t, alog_row, dt_row)


def _gla_kernel(q_ref, k_ref, v_ref, g_ref, sm_ref, w2_ref, b2_ref, nw_ref, y_ref,
                qe_s, ki_s, kd_s, dl_s, st_s, *, seq):
    c = GLA_CHUNK
    rb = ROW_BLOCK
    cpb = rb // c
    l_incl = _cumsum_matrix(rb, c)
    scale = GLA_DK ** -0.5

    pg = GLA_PREP_BLOCKS

    def prep(i, carry):
        blks = [i * pg + b for b in range(pg)]
        rws = [pl.ds(pl.multiple_of(blk * rb, rb), rb) for blk in blks]
        zs = [_dot(sm_ref[r, :].astype(BF16), w2_ref[...]) + b2_ref[...] for r in rws]
        las = [-_softplus(-z) * (1.0 / GLA_GATE_NORM) for z in zs]
        his = [la.astype(BF16) for la in las]
        los = [(la - hi.astype(F32)).astype(BF16) for la, hi in zip(las, his)]
        cums = [_dot(l_incl, hi) + _dot(l_incl, lo) for hi, lo in zip(his, los)]
        for blk, r, cum in zip(blks, rws, cums):
            lasts = [cum[j * c + c - 1:j * c + c, :] for j in range(cpb)]
            elast = jnp.concatenate([jnp.broadcast_to(jnp.exp(l), (c, GLA_DK)) for l in lasts], axis=0)
            q = q_ref[r, :].astype(F32)
            k = k_ref[r, :].astype(F32)
            ki = k * jnp.exp(-cum)
            qe_s[r, :] = (q * (jnp.exp(cum) * scale)).astype(BF16)
            ki_s[r, :] = ki.astype(BF16)
            kd_s[r, :] = (ki * elast).astype(BF16)
            for j in range(cpb):
                dl_s[blk * cpb + j] = elast[j * c:j * c + 8, :]
        return carry

    lax.fori_loop(0, seq // (rb * pg), prep, 0)

    st_s[...] = jnp.zeros_like(st_s)
    tr = lax.broadcasted_iota(jnp.int32, (c, c), 0)
    tc = lax.broadcasted_iota(jnp.int32, (c, c), 1)
    causal = tc <= tr
    grp = GLA_GROUP

    def step(i, carry):
        ns = [i * grp + j for j in range(grp)]
        ras = [pl.ds(pl.multiple_of(n * 2 * c, c), c) for n in ns]
        rbs = [pl.ds(pl.multiple_of(n * 2 * c + c, c), c) for n in ns]
        rabs = [pl.ds(pl.multiple_of(n * 2 * c, 2 * c), 2 * c) for n in ns]
        das = [dl_s[2 * n][0:1, :] for n in ns]
        dbs = [dl_s[2 * n + 1][0:1, :] for n in ns]
        qas = [qe_s[r, :] for r in ras]
        qbs = [qe_s[r, :] for r in rbs]
        kdas = [kd_s[r, :] for r in ras]
        v2s = [v_ref[r, :] for r in rabs]
        aas = [jnp.where(causal, _dot_nt(qa, ki_s[r, :]), 0.0).astype(BF16) for qa, r in zip(qas, ras)]
        bbs = [jnp.where(causal, _dot_nt(qb, ki_s[r, :]), 0.0).astype(BF16) for qb, r in zip(qbs, rbs)]
        bas = [_dot_nt(qb, kda).astype(BF16) for qb, kda in zip(qbs, kdas)]
        babs = [jnp.concatenate([ba, bb], axis=1) for ba, bb in zip(bas, bbs)]
        q2s = [jnp.concatenate([qa, (qb.astype(F32) * da).astype(BF16)], axis=0)
               for qa, qb, da in zip(qas, qbs, das)]
        kd2s = [jnp.concatenate([(kda.astype(F32) * db).astype(BF16), kd_s[r, :]], axis=0)
                for kda, db, r in zip(kdas, dbs, rbs)]
        upds = [_dot_tn(v2, kd2) for v2, kd2 in zip(v2s, kd2s)]
        st = st_s[...]
        sts = []
        for j in range(grp):
            sts.append(st.astype(BF16))
            st = st * (das[j] * dbs[j]) + upds[j]
        st_s[...] = st
        for j in range(grp):
            o2 = _dot_nt(q2s[j], sts[j])
            oa = o2[0:c, :] + _dot(aas[j], v2s[j][0:c, :])
            ob = o2[c:2 * c, :] + _dot(babs[j], v2s[j])
            ya = _gated_rmsnorm(oa, nw_ref[...], g_ref[ras[j], :].astype(F32))
            yb = _gated_rmsnorm(ob, nw_ref[...], g_ref[rbs[j], :].astype(F32))
            y_ref[ras[j], :] = ya.astype(y_ref.dtype)
            y_ref[rbs[j], :] = yb.astype(y_ref.dtype)
        return carry

    lax.fori_loop(0, seq // (2 * c * grp), step, 0)


def _gla(u, small, w2p, b2, norm_w, bsz, seq):
    t = u.shape[0]
    qb, vb = GLA_DK, GLA_DV
    return pl.pallas_call(
        functools.partial(_gla_kernel, seq=seq),
        grid=(bsz, GLA_HEADS),
        in_specs=[
            pl.BlockSpec((seq, qb), lambda b, h: (b, OFF_GLA_Q // qb + h)),
            pl.BlockSpec((seq, qb), lambda b, h: (b, OFF_GLA_K // qb + h)),
            pl.BlockSpec((seq, vb), lambda b, h: (b, OFF_GLA_V // vb + h)),
            pl.BlockSpec((seq, vb), lambda b, h: (b, OFF_GLA_G // vb + h)),
            pl.BlockSpec((seq, N_SMALL), lambda b, h: (b, 0)),
            pl.BlockSpec((N_SMALL, qb), lambda b, h: (0, h)),
            pl.BlockSpec((1, qb), lambda b, h: (0, h)),
            pl.BlockSpec((1, vb), lambda b, h: (0, 0)),
        ],
        out_specs=pl.BlockSpec((seq, vb), lambda b, h: (b, h)),
        out_shape=jax.ShapeDtypeStruct((t, GLA_HEADS * GLA_DV), BF16),
        scratch_shapes=[
            pltpu.VMEM((seq, qb), BF16),
            pltpu.VMEM((seq, qb), BF16),
            pltpu.VMEM((seq, qb), BF16),
            pltpu.VMEM((seq // GLA_CHUNK, 8, qb), F32),
            pltpu.VMEM((vb, qb), F32),
        ],
        compiler_params=pltpu.CompilerParams(
            dimension_semantics=("arbitrary", "arbitrary"), vmem_limit_bytes=VMEM_LIMIT),
        name="gla",
    )(u, u, u, u, small, w2p, b2, norm_w)


def _unit_lower_inverse(ms, eye, tr, tc):
    n = ms[0].shape[0]
    base = _same_chunk(tr, tc, INV_BASE)
    ps = [jnp.where(base, -m, 0.0) for m in ms]
    tinvs = [eye + p for p in ps]
    size = 2
    while size < INV_BASE:
        pbs = [p.astype(BF16) for p in ps]
        ps = [_dot(pb, pb) for pb in pbs]
        tinvs = [t + _dot(t.astype(BF16), p.astype(BF16)) for t, p in zip(tinvs, ps)]
        size *= 2
    blk = INV_BASE
    while blk < n:
        off = _same_chunk(tr, tc, 2 * blk) & jnp.logical_not(_same_chunk(tr, tc, blk))
        moffs = [jnp.where(off, m, 0.0).astype(BF16) for m in ms]
        tbs = [t.astype(BF16) for t in tinvs]
        mids = [_dot(tb, mo).astype(BF16) for tb, mo in zip(tbs, moffs)]
        tinvs = [t - _dot(mid, tb) for t, mid, tb in zip(tinvs, mids, tbs)]
        blk *= 2
    return tinvs


def _gdn_kernel(q_ref, k_ref, v_ref, z_ref, sm_ref, cwq_ref, cwk_ref, cwv_ref, nw_ref, y_ref,
                xq_s, xk_s, xv_s, q_s, k_s, v_s, bc_s, cc_s,
                u_s, wq_s, a_s, ws_s, bs_s, gl_s, *, seq):
    c = GDN_CHUNK
    nchunk = seq // c
    rb = ROW_BLOCK
    dk = GDN_DK
    scale = GDN_DK ** -0.5

    er = lax.broadcasted_iota(jnp.int32, (N_SMALL, 2 * dk), 0)
    ec = lax.broadcasted_iota(jnp.int32, (N_SMALL, 2 * dk), 1)
    sel_lane = jnp.where(ec < dk, LANE_B, LANE_A)
    tr = lax.broadcasted_iota(jnp.int32, (c, c), 0)
    tc = lax.broadcasted_iota(jnp.int32, (c, c), 1)
    lower = tc <= tr
    strict = tc < tr
    eye = jnp.where(tc == tr, 1.0, 0.0).astype(F32)

    def conv_silu(xs, cw_ref, lanes, r0):
        acc = None
        for j in range(CONV_K):
            lo = r0 + 8 - (CONV_K - 1) + j
            tap = xs[lo:lo + rb, :] * cw_ref[j:j + 1, lanes]
            acc = tap if acc is None else acc + tap
        return _silu(acc)

    def l2n(x):
        return x * lax.rsqrt(jnp.sum(x * x, axis=-1, keepdims=True) + NORM_EPS)

    for hp in range(GDN_HEADS_PER_STEP):
        lanes = slice(hp * dk, (hp + 1) * dk)
        head = pl.program_id(1) * GDN_HEADS_PER_STEP + hp

        for src, dst in ((q_ref, xq_s), (k_ref, xk_s), (v_ref, xv_s)):
            dst[0:8, :] = jnp.zeros((8, dk), F32)
            dst[8:8 + seq, :] = src[:, lanes].astype(F32)

        sel = jnp.where(er == sel_lane + head, 1.0, 0.0).astype(BF16)

        for i in range(seq // rb):
            r0 = i * rb
            rows = slice(r0, r0 + rb)
            q_s[rows, :] = l2n(conv_silu(xq_s, cwq_ref, lanes, r0)) * scale
            k_s[rows, :] = l2n(conv_silu(xk_s, cwk_ref, lanes, r0))
            v_s[rows, :] = conv_silu(xv_s, cwv_ref, lanes, r0)
            both = _dot_exact_rhs(sm_ref[rows, :], sel)
            bc_s[rows, :] = both[:, :dk]
            cc_s[rows, :] = both[:, dk:]

        for g0 in range(0, nchunk, GDN_PREP_GROUP):
            ns = list(range(g0, g0 + GDN_PREP_GROUP))
            rws = [slice(n * c, (n + 1) * c) for n in ns]
            qs = [q_s[r, :] for r in rws]
            ks = [k_s[r, :] for r in rws]
            vs = [v_s[r, :].astype(BF16) for r in rws]
            bcols = [bc_s[r, :] for r in rws]
            ccols = [cc_s[r, :] for r in rws]
            brows = [b.T for b in bcols]
            crows = [cc.T for cc in ccols]
            gams = [jnp.where(lower, jnp.exp(jnp.where(lower, cc - cr, 0.0)), 0.0)
                    for cc, cr in zip(ccols, crows)]
            kbs = [k.astype(BF16) for k in ks]
            kks = [_dot_nt(kb, kb) for kb in kbs]
            qks = [_dot_nt(q.astype(BF16), kb) for q, kb in zip(qs, kbs)]
            ms = [jnp.where(strict, b * kk * g, 0.0) for b, kk, g in zip(bcols, kks, gams)]
            tinvs = _unit_lower_inverse(ms, eye, tr, tc)
            tbs = [t * b for t, b in zip(tinvs, brows)]
            us = [_dot(tb.astype(BF16), v) for tb, v in zip(tbs, vs)]
            ws = [_dot((tb * jnp.exp(cr)).astype(BF16), kb).astype(BF16)
                  for tb, cr, kb in zip(tbs, crows, kbs)]
            lastrows = [cc[c - 1:c, :] for cc in ccols]
            kdts = [(k * jnp.exp(l - cc)).T.astype(BF16) for k, l, cc in zip(ks, lastrows, ccols)]
            wss = [_dot(kdt, w) for kdt, w in zip(kdts, ws)]
            bss = [_dot(kdt, u.astype(BF16)) for kdt, u in zip(kdts, us)]
            for j, n in enumerate(ns):
                u_s[hp, n] = us[j]
                wq_s[hp, n, 0:c, :] = ws[j]
                wq_s[hp, n, c:2 * c, :] = (qs[j] * jnp.exp(ccols[j])).astype(BF16)
                a_s[hp, n] = (qks[j] * gams[j]).astype(BF16)
                ws_s[hp, n] = wss[j].astype(BF16)
                bs_s[hp, n] = bss[j]
                gl_s[hp, n] = jnp.broadcast_to(jnp.exp(lastrows[j]), (8, GDN_DV))

    sts = [jnp.zeros((dk, GDN_DV), F32) for _ in range(GDN_HEADS_PER_STEP)]
    for n in range(nchunk):
        rows = slice(n * c, (n + 1) * c)
        for hp in range(GDN_HEADS_PER_STEP):
            lanes = slice(hp * dk, (hp + 1) * dk)
            sb = sts[hp].astype(BF16)
            sts[hp] = sts[hp] * gl_s[hp, n][0:1, :] - _dot(ws_s[hp, n], sb) + bs_s[hp, n]
            wq = _dot(wq_s[hp, n], sb)
            vb = (u_s[hp, n] - wq[0:c, :]).astype(BF16)
            o = wq[c:2 * c, :] + _dot(a_s[hp, n], vb)
            y = _gated_rmsnorm(o, nw_ref[...], z_ref[rows, lanes].astype(F32))
            y_ref[rows, lanes] = y.astype(y_ref.dtype)


def _gdn(u, small, cw8, norm_w, bsz, seq):
    t = u.shape[0]
    dk = GDN_DK
    hps = GDN_HEADS_PER_STEP
    wb = hps * dk
    nchunk = seq // GDN_CHUNK
    c = GDN_CHUNK
    conv_blocks = GDN_HEADS // hps
    return pl.pallas_call(
        functools.partial(_gdn_kernel, seq=seq),
        grid=(bsz, GDN_HEADS // hps),
        in_specs=[
            pl.BlockSpec((seq, wb), lambda b, p: (b, OFF_GDN_Q // wb + p)),
            pl.BlockSpec((seq, wb), lambda b, p: (b, OFF_GDN_K // wb + p)),
            pl.BlockSpec((seq, wb), lambda b, p: (b, OFF_GDN_V // wb + p)),
            pl.BlockSpec((seq, wb), lambda b, p: (b, OFF_GDN_Z // wb + p)),
            pl.BlockSpec((seq, N_SMALL), lambda b, p: (b, 0)),
            pl.BlockSpec((8, wb), lambda b, p: (0, p)),
            pl.BlockSpec((8, wb), lambda b, p: (0, conv_blocks + p)),
            pl.BlockSpec((8, wb), lambda b, p: (0, 2 * conv_blocks + p)),
            pl.BlockSpec((1, dk), lambda b, p: (0, 0)),
        ],
        out_specs=pl.BlockSpec((seq, wb), lambda b, p: (b, p)),
        out_shape=jax.ShapeDtypeStruct((t, GDN_HEADS * GDN_DV), BF16),
        scratch_shapes=[
            pltpu.VMEM((seq + 8, dk), F32),
            pltpu.VMEM((seq + 8, dk), F32),
            pltpu.VMEM((seq + 8, dk), F32),
            pltpu.VMEM((seq, dk), F32),
            pltpu.VMEM((seq, dk), F32),
            pltpu.VMEM((seq, dk), F32),
            pltpu.VMEM((seq, dk), F32),
            pltpu.VMEM((seq, dk), F32),
            pltpu.VMEM((hps, nchunk, c, dk), F32),
            pltpu.VMEM((hps, nchunk, 2 * c, dk), BF16),
            pltpu.VMEM((hps, nchunk, c, c), BF16),
            pltpu.VMEM((hps, nchunk, dk, dk), BF16),
            pltpu.VMEM((hps, nchunk, dk, dk), F32),
            pltpu.VMEM((hps, nchunk, 8, dk), F32),
        ],
        compiler_params=pltpu.CompilerParams(
            dimension_semantics=("arbitrary", "arbitrary"), vmem_limit_bytes=VMEM_LIMIT),
        name="gdn",
    )(u, u, u, u, small, cw8, cw8, cw8, norm_w)


def _merge_kernel(yg_ref, yd_ref, wa_ref, wb_ref, ga_ref, gb_ref, o_ref):
    pa = _dot(yg_ref[...], wa_ref[...])
    pb = _dot(yd_ref[...], wb_ref[...])
    ga = _sigmoid(ga_ref[...].astype(F32))
    gb = _sigmoid(gb_ref[...].astype(F32))
    o_ref[...] = (ga * pa + gb * pb).astype(o_ref.dtype)


def _merge(y_gla, y_gdn, w_branch, u):
    t, d = y_gla.shape
    tm, tn = 1024, 1024
    return pl.pallas_call(
        _merge_kernel,
        grid=(t // tm, d // tn),
        in_specs=[
            pl.BlockSpec((tm, d), lambda m, n: (m, 0)),
            pl.BlockSpec((tm, d), lambda m, n: (m, 0)),
            pl.BlockSpec((None, d, tn), lambda m, n: (0, 0, n)),
            pl.BlockSpec((None, d, tn), lambda m, n: (1, 0, n)),
            pl.BlockSpec((tm, tn), lambda m, n: (m, OFF_MERGE // tn + n)),
            pl.BlockSpec((tm, tn), lambda m, n: (m, (OFF_MERGE + D_MODEL) // tn + n)),
        ],
        out_specs=pl.BlockSpec((tm, tn), lambda m, n: (m, n)),
        out_shape=jax.ShapeDtypeStruct((t, d), BF16),
        compiler_params=pltpu.CompilerParams(
            dimension_semantics=("arbitrary", "arbitrary"), vmem_limit_bytes=VMEM_LIMIT),
        name="merge",
    )(y_gla, y_gdn, w_branch, w_branch, u, u)


def _outnorm_kernel(m_ref, w_ref, x_ref, gate_ref, lg_ref, lb_ref, o_ref, *, alpha):
    half = m_ref.shape[0] // 2
    outs = [_dot(m_ref[p * half:(p + 1) * half, :], w_ref[...]) for p in range(2)]
    for p, out in enumerate(outs):
        rows = slice(p * half, (p + 1) * half)
        r = alpha * x_ref[rows, :] + gate_ref[0] * out
        mu = jnp.mean(r, axis=-1, keepdims=True)
        rc = r - mu
        var = jnp.mean(rc * rc, axis=-1, keepdims=True)
        o_ref[rows, :] = rc * lax.rsqrt(var + LN_EPS) * lg_ref[...] + lb_ref[...]


def _outnorm(merged, w_out, x2, ada3, ln_g, ln_b, seq, alpha):
    t, d = x2.shape
    tm = 512
    per_b = seq // tm
    return pl.pallas_call(
        functools.partial(_outnorm_kernel, alpha=alpha),
        grid=(t // tm,),
        in_specs=[
            pl.BlockSpec((tm, d), lambda m: (m, 0)),
            pl.BlockSpec((d, d), lambda m: (0, 0)),
            pl.BlockSpec((tm, d), lambda m: (m, 0)),
            pl.BlockSpec((1, 1, d), lambda m: (m // per_b, 0, 2)),
            pl.BlockSpec((1, d), lambda m: (0, 0)),
            pl.BlockSpec((1, d), lambda m: (0, 0)),
        ],
        out_specs=pl.BlockSpec((tm, d), lambda m: (m, 0)),
        out_shape=jax.ShapeDtypeStruct((t, d), F32),
        compiler_params=pltpu.CompilerParams(
            dimension_semantics=("arbitrary",), vmem_limit_bytes=VMEM_LIMIT),
        name="outnorm",
    )(merged, w_out, x2, ada3, ln_g, ln_b)


def _pack_w_in(w_in):
    w_t = jnp.transpose(w_in).astype(BF16)
    glr = OFF_GDN_Q
    db_ = glr + GLA_RANK + 4 * GDN_HEADS * GDN_DK
    mg = db_ + 2 * GDN_HEADS
    assert mg + 2 * D_MODEL == w_in.shape[1]
    pad = jnp.zeros((N_SMALL - GLA_RANK - 2 * GDN_HEADS, w_in.shape[0]), BF16)
    small_t = jnp.concatenate([w_t[glr:glr + GLA_RANK], w_t[db_:mg], pad], axis=0)
    return w_t, small_t


def _layer(x, c, w_ada, b_ada, w_in, w_gk2, b_gk2, conv_w, a_log, dt_bias,
           gla_norm_w, gdn_norm_w, w_branch, w_out, ln_g, ln_b, alpha):
    bsz, seq, d = x.shape
    t = bsz * seq
    x2 = x.reshape(t, d)

    ada = _ada(c, w_ada, b_ada)
    ada3 = ada.reshape(bsz, 1, 3 * d)

    w_t, w_small_t = _pack_w_in(w_in)
    lane_pad = lambda v: jnp.zeros((1, N_SMALL), F32).at[0, LANE_A:LANE_A + GDN_HEADS].set(v)
    u, small = _inproj(x2, ada3, w_t, w_small_t, lane_pad(a_log), lane_pad(dt_bias), seq)

    w2p = jnp.concatenate(
        [w_gk2, jnp.zeros((N_SMALL - GLA_RANK, w_gk2.shape[1]), w_gk2.dtype)], axis=0).astype(BF16)
    y_gla = _gla(u, small, w2p, b_gk2.reshape(1, -1), gla_norm_w.reshape(1, -1), bsz, seq)

    cw8 = jnp.concatenate([conv_w, jnp.zeros((8 - CONV_K, conv_w.shape[1]), conv_w.dtype)], axis=0)
    y_gdn = _gdn(u, small, cw8, gdn_norm_w.reshape(1, -1), bsz, seq)

    merged = _merge(y_gla, y_gdn, w_branch.astype(BF16), u)
    out = _outnorm(merged, w_out.astype(BF16), x2, ada3, ln_g.reshape(1, -1), ln_b.reshape(1, -1),
                   seq, alpha)
    return out.reshape(bsz, seq, d)


def kernel(x, c, w_ada, b_ada, w_in, w_gk2, b_gk2, conv_w, a_log, dt_bias, gla_norm_w, gdn_norm_w,
           w_branch, w_out, ln_g, ln_b):
    depth = w_ada.shape[0]
    alpha = (2 * depth) ** 0.25
    for l in range(depth):
        x = _layer(x, c, w_ada[l], b_ada[l], w_in[l], w_gk2[l], b_gk2[l], conv_w[l], a_log[l],
                   dt_bias[l], gla_norm_w[l], gdn_norm_w[l], w_branch[l], w_out[l], ln_g[l], ln_b[l],
                   alpha)
    return x
```
